```python
import math
import jax, jax.numpy as jnp
from jax import lax
import numpy as np

D_MODEL = 2048
BATCH = 2
SEQ = 8192
DEPTH = 4

GRID_W = 64
CTX_LEN = 256

HEAD_DIM = 128
N_Q_HEADS = 8
N_KV_HEADS = 2
Q_PER_KV = N_Q_HEADS // N_KV_HEADS
ATTN_WIDTH = N_Q_HEADS * HEAD_DIM
KV_WIDTH = N_KV_HEADS * HEAD_DIM
ATTN_SCALE = HEAD_DIM ** -0.5
ROPE_THETA = 10000.0
Q_BLOCK = 128

HYENA_WIDTH = D_MODEL - ATTN_WIDTH
HYENA_ORDER = 2
HYENA_SHORT_K = 3
HYENA_EMB_DIM = 33
HYENA_FILTER_HIDDEN = 64
HYENA_FAST_DECAY_PCT = 0.3
HYENA_SLOW_DECAY_PCT = 1.5
HYENA_TARGET = 1e-2

IN_WIDTH_EVEN = ATTN_WIDTH + 2 * KV_WIDTH + (HYENA_ORDER + 1) * HYENA_WIDTH
OUT_WIDTH_EVEN = ATTN_WIDTH + HYENA_WIDTH

CONF_WIDTH = D_MODEL
CONF_K = 31

N_EXPERTS = 16
N_GROUPS = 4
EXPERTS_PER_GROUP = N_EXPERTS // N_GROUPS
TOPK_GROUPS = 1
TOP_K = 2
D_EXPERT = 1024

DEEPNORM_ALPHA = (2 * DEPTH) ** 0.25
DEEPNORM_BETA = (8 * DEPTH) ** -0.25

N_EVEN = (DEPTH + 1) // 2
N_ODD = DEPTH // 2
N_MOD = 6
NORM_EPS = 1e-6

kernel_name = "hybrid_attn_hyena_conformer_moe_dit"

F32 = jnp.float32


def layer_norm(x, g, b):
    xf = x.astype(F32)
    mu = xf.mean(-1, keepdims=True)
    var = jnp.mean(jnp.square(xf - mu), -1, keepdims=True)
    return ((xf - mu) * lax.rsqrt(var + NORM_EPS) * g.astype(F32) + b.astype(F32)).astype(x.dtype)


def rms_norm(x, g):
    xf = x.astype(F32)
    return (xf * lax.rsqrt(jnp.mean(jnp.square(xf), -1, keepdims=True) + NORM_EPS) * g.astype(F32)).astype(x.dtype)


def dwconv(x, w, b):
    k = w.shape[0]
    pad = (k - 1) // 2
    y = lax.conv_general_dilated(x, w[:, None, :].astype(x.dtype), window_strides=(1,),
                                 padding=[(pad, pad)], dimension_numbers=("NWC", "WIO", "NWC"),
                                 feature_group_count=x.shape[-1])
    return y + b.astype(x.dtype)


def rope_2d(n_tokens):
    rows = n_tokens // GRID_W
    row = jnp.repeat(jnp.arange(rows, dtype=F32), GRID_W)
    col = jnp.tile(jnp.arange(GRID_W, dtype=F32), rows)
    half = HEAD_DIM // 2
    inv = ROPE_THETA ** (-jnp.arange(0, half, 2, dtype=F32) / half)
    ang = jnp.concatenate([row[:, None] * inv, col[:, None] * inv], -1)
    return jnp.cos(ang), jnp.sin(ang)


def apply_rope(x, cos, sin):
    xf = x.astype(F32).reshape(x.shape[:-1] + (HEAD_DIM // 2, 2))
    x0, x1 = xf[..., 0], xf[..., 1]
    c, s = cos[:, None, :], sin[:, None, :]
    return jnp.stack([x0 * c - x1 * s, x0 * s + x1 * c], -1).reshape(x.shape).astype(x.dtype)


def attend(qb, k, v):
    s = jnp.einsum("bqhgd,bkhd->bhgqk", qb, k, preferred_element_type=F32) * ATTN_SCALE
    p = jax.nn.softmax(s, axis=-1)
    return jnp.einsum("bhgqk,bkhd->bqhgd", p.astype(v.dtype), v)


def latent_attention(q, k, v, kc, vc):
    B, S = q.shape[:2]
    k_all = jnp.concatenate([k, kc], axis=1)
    v_all = jnp.concatenate([v, vc], axis=1)
    qb = q.reshape(B, S // Q_BLOCK, Q_BLOCK, N_KV_HEADS, Q_PER_KV, HEAD_DIM)
    qb = jnp.moveaxis(qb, 1, 0)
    o = lax.map(lambda qi: attend(qi, k_all, v_all), qb)
    return jnp.moveaxis(o, 0, 1).reshape(B, S, ATTN_WIDTH)


def context_attention(qc, kc, vc):
    B, L = qc.shape[:2]
    o = attend(qc.reshape(B, L, N_KV_HEADS, Q_PER_KV, HEAD_DIM), kc, vc)
    return o.reshape(B, L, ATTN_WIDTH)


def hyena_filter_spectrum(L, f_w1, f_b1, f_w2, f_b2, f_w3, f_freq):
    t = jnp.arange(L, dtype=F32)
    t01 = t / max(L - 1, 1)
    bands = (HYENA_EMB_DIM - 1) // 2
    f = jnp.linspace(1e-4, bands - 1, bands, dtype=F32)
    fw = (2.0 * math.pi * t / L)[:, None] * f[None, :]
    z = jnp.concatenate([t01[:, None], jnp.cos(fw), -jnp.sin(fw)], -1)
    freq = f_freq.astype(F32)
    a = jnp.sin(freq * (z @ f_w1.astype(F32) + f_b1.astype(F32)))
    a = jnp.sin(freq * (a @ f_w2.astype(F32) + f_b2.astype(F32)))
    h = (a @ f_w3.astype(F32)).reshape(L, 2, HYENA_ORDER, HYENA_WIDTH)
    max_decay = math.log(HYENA_TARGET) / HYENA_FAST_DECAY_PCT
    min_decay = math.log(HYENA_TARGET) / HYENA_SLOW_DECAY_PCT
    deltas = jnp.abs(jnp.linspace(min_decay, max_decay, HYENA_WIDTH, dtype=F32))
    h = h * jnp.exp(-t01[:, None] * deltas[None, :])[:, None, None, :]
    fwd, bwd = h[:, 0], h[:, 1]
    l1 = jnp.abs(fwd).sum(0) + jnp.abs(bwd[1:]).sum(0)
    k_circ = jnp.concatenate([fwd, jnp.zeros((1,) + fwd.shape[1:], F32), jnp.flip(bwd[1:], 0)], 0) / l1
    return jnp.fft.rfft(k_circ, axis=0)


def long_conv(u, kf, d):
    L = u.shape[1]
    uf = u.astype(F32)
    y = jnp.fft.irfft(jnp.fft.rfft(uf, n=2 * L, axis=1) * kf[None], n=2 * L, axis=1)[:, :L]
    return (y + uf * d.astype(F32)).astype(u.dtype)


def hyena(u, conv_w, conv_b, f_w1, f_b1, f_w2, f_b2, f_w3, f_freq, skip):
    L = u.shape[1]
    u = dwconv(u, conv_w, conv_b)
    v, x1, x2 = jnp.split(u, 3, axis=-1)
    kf = hyena_filter_spectrum(L, f_w1, f_b1, f_w2, f_b2, f_w3, f_freq)
    z = x1 * long_conv(v, kf[:, 0], skip[0])
    return x2 * long_conv(z, kf[:, 1], skip[1])


def even_mixer(h, hc, ctx_out, cos, sin, w_in, q_g, k_g, conv_w, conv_b,
               f_w1, f_b1, f_w2, f_b2, f_w3, f_freq, skip, w_out):
    B, S, _ = h.shape
    Lc = hc.shape[1]
    splits = [ATTN_WIDTH, ATTN_WIDTH + KV_WIDTH, ATTN_WIDTH + 2 * KV_WIDTH]
    q, k, v, u = jnp.split(h @ w_in, splits, axis=-1)
    q = apply_rope(rms_norm(q.reshape(B, S, N_Q_HEADS, HEAD_DIM), q_g), cos, sin)
    k = apply_rope(rms_norm(k.reshape(B, S, N_KV_HEADS, HEAD_DIM), k_g), cos, sin)
    v = v.reshape(B, S, N_KV_HEADS, HEAD_DIM)
    if ctx_out:
        qc, kc, vc, uc = jnp.split(hc @ w_in, splits, axis=-1)
    else:
        kc, vc = jnp.split(hc @ w_in[:, ATTN_WIDTH:ATTN_WIDTH + 2 * KV_WIDTH], 2, axis=-1)
    kc = rms_norm(kc.reshape(B, Lc, N_KV_HEADS, HEAD_DIM), k_g)
    vc = vc.reshape(B, Lc, N_KV_HEADS, HEAD_DIM)
    att = latent_attention(q, k, v, kc, vc)
    hy = hyena(u, conv_w, conv_b, f_w1, f_b1, f_w2, f_b2, f_w3, f_freq, skip)
    y = jnp.concatenate([att, hy], -1) @ w_out
    if not ctx_out:
        return y, None
    qc = rms_norm(qc.reshape(B, Lc, N_Q_HEADS, HEAD_DIM), q_g)
    att_c = context_attention(qc, kc, vc)
    hy_c = hyena(uc, conv_w, conv_b, f_w1, f_b1, f_w2, f_b2, f_w3, f_freq, skip)
    yc = jnp.concatenate([att_c, hy_c], -1) @ w_out
    return y, yc


def conformer_conv(h, in_w, in_b, dw_w, dw_b, n_g, n_b, out_w, out_b):
    a = h @ in_w + in_b
    a = a[..., :CONF_WIDTH] * jax.nn.sigmoid(a[..., CONF_WIDTH:])
    a = dwconv(a, dw_w, dw_b)
    a = jax.nn.silu(layer_norm(a, n_g, n_b))
    return a @ out_w + out_b


def moe(h, router_w, router_b, w1, w3, w2):
    T = h.shape[0]
    aff = jax.nn.sigmoid((h @ router_w).astype(F32))
    sel = (aff + router_b.astype(F32)).reshape(T, N_GROUPS, EXPERTS_PER_GROUP)
    grp_score = lax.top_k(sel, TOP_K)[0].sum(-1)
    _, top_g = lax.top_k(grp_score, TOPK_GROUPS)
    g_mask = jax.nn.one_hot(top_g, N_GROUPS, dtype=F32).sum(1) > 0
    masked = jnp.where(g_mask[:, :, None], sel, -jnp.inf).reshape(T, N_EXPERTS)
    _, top_e = lax.top_k(masked, TOP_K)
    w = jnp.take_along_axis(aff, top_e, axis=1)
    w = w / w.sum(-1, keepdims=True)
    gates = jnp.einsum("tk,tke->te", w, jax.nn.one_hot(top_e, N_EXPERTS, dtype=F32)).astype(h.dtype)
    out = jnp.zeros_like(h)
    for e in range(N_EXPERTS):
        a = jax.nn.silu(h @ w1[e]) * (h @ w3[e])
        out = out + gates[:, e:e + 1] * (a @ w2[e])
    return out


def setup_inputs(seed: int = 0) -> dict:
    key = jax.random.key(seed)
    ks = iter(jax.random.split(key, 40))

    def nrm(shape, scale):
        return jax.random.normal(next(ks), shape, F32) * scale

    D = D_MODEL
    return {
        "x": nrm((BATCH, SEQ, D), 1.0),
        "c": nrm((BATCH, D), 1.0),
        "ctx": nrm((BATCH, CTX_LEN, D), 1.0),
        "c_ctx": nrm((D,), 1.0),
        "mod_w": nrm((DEPTH, D, N_MOD * D), 0.5 * D ** -0.5),
        "mod_b": nrm((DEPTH, N_MOD * D), 0.01),
        "ln_g": 1.0 + nrm((DEPTH, 2, D), 0.02),
        "ln_b": nrm((DEPTH, 2, D), 0.01),
        "mix_in_w": nrm((N_EVEN, D, IN_WIDTH_EVEN), D ** -0.5),
        "q_norm_g": 1.0 + nrm((N_EVEN, HEAD_DIM), 0.02),
        "k_norm_g": 1.0 + nrm((N_EVEN, HEAD_DIM), 0.02),
        "hy_conv_w": nrm((N_EVEN, HYENA_SHORT_K, 3 * HYENA_WIDTH), HYENA_SHORT_K ** -0.5),
        "hy_conv_b": nrm((N_EVEN, 3 * HYENA_WIDTH), 0.01),
        "hy_w1": nrm((N_EVEN, HYENA_EMB_DIM, HYENA_FILTER_HIDDEN), HYENA_EMB_DIM ** -0.5),
        "hy_b1": nrm((N_EVEN, HYENA_FILTER_HIDDEN), 0.01),
        "hy_w2": nrm((N_EVEN, HYENA_FILTER_HIDDEN, HYENA_FILTER_HIDDEN), HYENA_FILTER_HIDDEN ** -0.5),
        "hy_b2": nrm((N_EVEN, HYENA_FILTER_HIDDEN), 0.01),
        "hy_w3": nrm((N_EVEN, HYENA_FILTER_HIDDEN, 2 * HYENA_ORDER * HYENA_WIDTH), HYENA_FILTER_HIDDEN ** -0.5),
        "hy_freq": 1.0 + nrm((N_EVEN, HYENA_FILTER_HIDDEN), 0.01),
        "hy_skip": nrm((N_EVEN, HYENA_ORDER, HYENA_WIDTH), 0.1),
        "mix_out_w": nrm((N_EVEN, OUT_WIDTH_EVEN, D), OUT_WIDTH_EVEN ** -0.5 * DEEPNORM_BETA),
        "cf_in_w": nrm((N_ODD, D, 2 * CONF_WIDTH), D ** -0.5),
        "cf_in_b": nrm((N_ODD, 2 * CONF_WIDTH), 0.01),
        "cf_dw_w": nrm((N_ODD, CONF_K, CONF_WIDTH), CONF_K ** -0.5),
        "cf_dw_b": nrm((N_ODD, CONF_WIDTH), 0.01),
        "cf_ln_g": 1.0 + nrm((N_ODD, CONF_WIDTH), 0.02),
        "cf_ln_b": nrm((N_ODD, CONF_WIDTH), 0.01),
        "cf_out_w": nrm((N_ODD, CONF_WIDTH, D), CONF_WIDTH ** -0.5 * DEEPNORM_BETA),
        "cf_out_b": nrm((N_ODD, D), 0.01),
        "router_w": nrm((D, N_EXPERTS), D ** -0.5),
        "router_b": nrm((N_EXPERTS,), 0.01),
        "exp_w1": nrm((DEPTH, N_EXPERTS, D, D_EXPERT), D ** -0.5),
        "exp_w3": nrm((DEPTH, N_EXPERTS, D, D_EXPERT), D ** -0.5),
        "exp_w2": nrm((DEPTH, N_EXPERTS, D_EXPERT, D), D_EXPERT ** -0.5 * DEEPNORM_BETA),
    }


def reference(x, c, ctx, c_ctx, mod_w, mod_b, ln_g, ln_b, mix_in_w, q_norm_g, k_norm_g,
              hy_conv_w, hy_conv_b, hy_w1, hy_b1, hy_w2, hy_b2, hy_w3, hy_freq, hy_skip,
              mix_out_w, cf_in_w, cf_in_b, cf_dw_w, cf_dw_b, cf_ln_g, cf_ln_b, cf_out_w, cf_out_b,
              router_w, router_b, exp_w1, exp_w3, exp_w2):
    B, S, D = x.shape
    cos, sin = rope_2d(S)
    sc = jax.nn.silu(c)
    scc = jax.nn.silu(c_ctx)
    for l in range(DEPTH):
        ctx_needed = any(j > l and j % 2 == 0 for j in range(DEPTH))
        mod = sc @ mod_w[l] + mod_b[l]
        sh1, sc1, g1, sh2, sc2, g2 = jnp.split(mod[:, None, :], N_MOD, axis=-1)
        csh1, csc1, cg1, csh2, csc2, cg2 = jnp.split(scc @ mod_w[l] + mod_b[l], N_MOD, axis=-1)

        h = x * (1.0 + sc1) + sh1
        hc = ctx * (1.0 + csc1) + csh1
        if l % 2 == 0:
            i = l // 2
            y, yc = even_mixer(h, hc, ctx_needed, cos, sin, mix_in_w[i], q_norm_g[i], k_norm_g[i],
                               hy_conv_w[i], hy_conv_b[i], hy_w1[i], hy_b1[i], hy_w2[i], hy_b2[i],
                               hy_w3[i], hy_freq[i], hy_skip[i], mix_out_w[i])
        else:
            i = l // 2
            cf = (cf_in_w[i], cf_in_b[i], cf_dw_w[i], cf_dw_b[i], cf_ln_g[i], cf_ln_b[i], cf_out_w[i], cf_out_b[i])
            y = conformer_conv(h, *cf)
            yc = conformer_conv(hc, *cf) if ctx_needed else None
        x = layer_norm(DEEPNORM_ALPHA * x + g1 * y, ln_g[l, 0], ln_b[l, 0])
        if ctx_needed:
            ctx = layer_norm(DEEPNORM_ALPHA * ctx + cg1 * yc, ln_g[l, 0], ln_b[l, 0])

        tokens = (x * (1.0 + sc2) + sh2).reshape(-1, D)
        if ctx_needed:
            tokens = jnp.concatenate([tokens, (ctx * (1.0 + csc2) + csh2).reshape(-1, D)], 0)
        out = moe(tokens, router_w, router_b, exp_w1[l], exp_w3[l], exp_w2[l])
        x = layer_norm(DEEPNORM_ALPHA * x + g2 * out[:B * S].reshape(B, S, D), ln_g[l, 1], ln_b[l, 1])
        if ctx_needed:
            yc = out[B * S:].reshape(ctx.shape)
            ctx = layer_norm(DEEPNORM_ALPHA * ctx + cg2 * yc, ln_g[l, 1], ln_b[l, 1])
    return x
```

```python
import functools
import math

import jax
import jax.numpy as jnp
from jax import lax
from jax.experimental import pallas as pl
from jax.experimental.pallas import tpu as pltpu

F32 = jnp.float32
BF16 = jnp.bfloat16
U32 = jnp.uint32
I32 = jnp.int32
HIGHEST = lax.Precision.HIGHEST
SDS = jax.ShapeDtypeStruct
BS = pl.BlockSpec

D_MODEL = 2048
DEPTH = 4
GRID_W = 64
HEAD_DIM = 128
N_Q_HEADS = 8
N_KV_HEADS = 2
Q_PER_KV = N_Q_HEADS // N_KV_HEADS
ATTN_WIDTH = N_Q_HEADS * HEAD_DIM
KV_WIDTH = N_KV_HEADS * HEAD_DIM
ATTN_SCALE = HEAD_DIM ** -0.5
ROPE_THETA = 10000.0
HYENA_WIDTH = D_MODEL - ATTN_WIDTH
HYENA_EMB_DIM = 33
HYENA_FILTER_HIDDEN = 64
HYENA_FAST_DECAY_PCT = 0.3
HYENA_SLOW_DECAY_PCT = 1.5
HYENA_TARGET = 1e-2
U_OFFSET = ATTN_WIDTH + 2 * KV_WIDTH
CONF_K = 31
N_EXPERTS = 16
N_GROUPS = 4
EXPERTS_PER_GROUP = 4
D_EXPERT = 1024
DEEPNORM_ALPHA = (2 * DEPTH) ** 0.25
N_MOD = 6
NORM_EPS = 1e-6

LANES = 128
SUBLANES = 8
VMEM_LIMIT_BYTES = 56 * 1024 * 1024

LN_ROWS = 256
MM_ROWS = 1024
GMM_ROWS = 512
GMM_HALF = D_EXPERT // 2
FFT_RADIX = 128
FFT_GROUP = 8
ATT_Q_ROWS = 256
HALO = 16


def _params(semantics):
    return pltpu.CompilerParams(dimension_semantics=semantics, vmem_limit_bytes=VMEM_LIMIT_BYTES)


def _mod_kernel(c_ref, w_ref, b_ref, o_ref):
    c = c_ref[...]
    s = c * jax.nn.sigmoid(c)
    o_ref[0] = jnp.dot(s, w_ref[0], precision=HIGHEST, preferred_element_type=F32) + b_ref[0]


def mod_vectors(cvec, mod_w, mod_b):
    depth, d, n = mod_w.shape
    tn = 1024
    return pl.pallas_call(
        _mod_kernel,
        out_shape=SDS((depth, SUBLANES, n), F32),
        grid=(depth, n // tn),
        in_specs=[BS((SUBLANES, d), lambda l, j: (0, 0)),
                  BS((1, d, tn), lambda l, j: (l, 0, j)),
                  BS((1, 1, tn), lambda l, j: (l, 0, j))],
        out_specs=BS((1, SUBLANES, tn), lambda l, j: (l, 0, j)),
        compiler_params=_params(("parallel", "parallel")),
        name="mod_vectors",
    )(cvec, mod_w, mod_b.reshape(depth, 1, n))


def _modulate_kernel(x_ref, sc_ref, sh_ref, o_ref):
    o_ref[0] = (x_ref[0] * (1.0 + sc_ref[0]) + sh_ref[0]).astype(o_ref.dtype)


def modulate(x, sc, sh):
    b, s, d = x.shape
    ts = min(LN_ROWS, s)
    vec = BS((1, 1, d), lambda bi, i: (bi, 0, 0))
    return pl.pallas_call(
        _modulate_kernel,
        out_shape=SDS((b, s, d), BF16),
        grid=(b, s // ts),
        in_specs=[BS((1, ts, d), lambda bi, i: (bi, i, 0)), vec, vec],
        out_specs=BS((1, ts, d), lambda bi, i: (bi, i, 0)),
        compiler_params=_params(("parallel", "parallel")),
        name="modulate",
    )(x, sc, sh)


def _mm_kernel(*refs, k_sizes, has_bias, glu):
    n_a = len(k_sizes)
    a_refs = refs[:n_a]
    pos = n_a
    w_refs = refs[pos:pos + (2 if glu else 1)]
    pos += len(w_refs)
    b_refs = ()
    if has_bias:
        b_refs = refs[pos:pos + len(w_refs)]
        pos += len(w_refs)
    o_ref = refs[pos]
    wbf_refs = refs[pos + 1:]

    @pl.when(pl.program_id(1) == 0)
    def _():
        for w_ref, wbf in zip(w_refs, wbf_refs):
            wbf[...] = w_ref[0].astype(BF16)

    def linear(which):
        acc = None
        off = 0
        for a_ref, k in zip(a_refs, k_sizes):
            part = jnp.dot(a_ref[...].astype(BF16), wbf_refs[which][off:off + k, :],
                           preferred_element_type=F32)
            acc = part if acc is None else acc + part
            off += k
        if has_bias:
            acc = acc + b_refs[which][0]
        return acc

    out = linear(0)
    if glu:
        out = out * jax.nn.sigmoid(linear(1))
    o_ref[...] = out.astype(o_ref.dtype)


def matmul(a_list, w, layer, *, n_out, col_off=0, bias=None, glu=False, out_dtype=F32, tn=512):
    m = a_list[0].shape[0]
    k_sizes = tuple(a.shape[1] for a in a_list)
    k_total = sum(k_sizes)
    assert w.shape[1] == k_total and n_out % tn == 0 and col_off % tn == 0
    tm = min(MM_ROWS, m)
    assert m % tm == 0
    nb = n_out // tn
    cb = col_off // tn
    in_specs = [BS((tm, k), lambda j, i: (i, 0)) for k in k_sizes]
    args = list(a_list)
    n_w = 2 if glu else 1
    for h in range(n_w):
        in_specs.append(BS((1, k_total, tn), lambda j, i, h=h: (layer, 0, cb + h * nb + j)))
        args.append(w)
    if bias is not None:
        bias3 = bias.reshape(bias.shape[0], 1, bias.shape[1])
        for h in range(n_w):
            in_specs.append(BS((1, 1, tn), lambda j, i, h=h: (layer, 0, cb + h * nb + j)))
            args.append(bias3)
    return pl.pallas_call(
        functools.partial(_mm_kernel, k_sizes=k_sizes, has_bias=bias is not None, glu=glu),
        out_shape=SDS((m, n_out), out_dtype),
        grid=(nb, m // tm),
        in_specs=in_specs,
        out_specs=BS((tm, tn), lambda j, i: (i, j)),
        scratch_shapes=[pltpu.VMEM((k_total, tn), BF16) for _ in range(n_w)],
        compiler_params=_params(("parallel", "arbitrary")),
        name="matmul_glu" if glu else "matmul",
    )(*args)


def _qkv_kernel(*refs, rope, n_q_heads, n_k_heads):
    it = iter(refs)
    q_ref = next(it) if n_q_heads else None
    kv_ref = next(it)
    qg_ref = next(it)
    kg_ref = next(it)
    cos_ref = sin_ref = None
    if rope:
        cos_ref = next(it)
        sin_ref = next(it)
    qo_ref = next(it) if n_q_heads else None
    ko_ref = next(it)
    vo_ref = next(it)

    def prep(xh, g, scale):
        ms = jnp.mean(xh * xh, axis=-1, keepdims=True)
        xn = xh * lax.rsqrt(ms + NORM_EPS) * g
        if rope:
            xn = xn * cos_ref[...] + pltpu.roll(xn, HEAD_DIM // 2, 1) * sin_ref[...]
        if scale != 1.0:
            xn = xn * scale
        return xn.astype(BF16)

    for h in range(n_q_heads):
        sl = slice(h * HEAD_DIM, (h + 1) * HEAD_DIM)
        qo_ref[0, :, sl] = prep(q_ref[0, :, sl], qg_ref[...], ATTN_SCALE)
    for h in range(n_k_heads):
        sl = slice(h * HEAD_DIM, (h + 1) * HEAD_DIM)
        ko_ref[0, :, sl] = prep(kv_ref[0, :, sl], kg_ref[...], 1.0)
    vo_ref[0] = kv_ref[0, :, KV_WIDTH:].astype(BF16)


def qkv_prepare(proj, q_gain, k_gain, cos_t, sin_t, *, rope, with_q):
    b, s, _ = proj.shape
    ts = min(512, s)
    n_q = N_Q_HEADS if with_q else 0
    in_specs, args = [], []
    if with_q:
        in_specs.append(BS((1, ts, ATTN_WIDTH), lambda bi, i: (bi, i, 0)))
        args.append(proj)
    in_specs.append(BS((1, ts, 2 * KV_WIDTH), lambda bi, i: (bi, i, ATTN_WIDTH // (2 * KV_WIDTH))))
    args.append(proj)
    gain = BS((1, HEAD_DIM), lambda bi, i: (0, 0))
    in_specs += [gain, gain]
    args += [q_gain, k_gain]
    if rope:
        tab = BS((ts, HEAD_DIM), lambda bi, i: (i, 0))
        in_specs += [tab, tab]
        args += [cos_t, sin_t]
    out_shape, out_specs = [], []
    if with_q:
        out_shape.append(SDS((b, s, ATTN_WIDTH), BF16))
        out_specs.append(BS((1, ts, ATTN_WIDTH), lambda bi, i: (bi, i, 0)))
    for _ in range(2):
        out_shape.append(SDS((b, s, KV_WIDTH), BF16))
        out_specs.append(BS((1, ts, KV_WIDTH), lambda bi, i: (bi, i, 0)))
    res = pl.pallas_call(
        functools.partial(_qkv_kernel, rope=rope, n_q_heads=n_q, n_k_heads=N_KV_HEADS),
        out_shape=out_shape,
        grid=(b, s // ts),
        in_specs=in_specs,
        out_specs=out_specs,
        compiler_params=_params(("parallel", "parallel")),
        name="qkv_prepare",
    )(*args)
    return res if with_q else (None,) + tuple(res)


def _attn_kernel(q_ref, k_ref, v_ref, o_ref, *, tk):
    tq = q_ref.shape[1]
    skv = k_ref.shape[1]
    rows = Q_PER_KV * tq
    qs = jnp.concatenate([q_ref[0, :, h * HEAD_DIM:(h + 1) * HEAD_DIM] for h in range(Q_PER_KV)], axis=0)

    def step(c, carry):
        m, l, acc = carry
        start = pl.multiple_of(c * tk, tk)
        kc = k_ref[0, pl.ds(start, tk), :]
        vc = v_ref[0, pl.ds(start, tk), :]
        s = lax.dot_general(qs, kc, (((1,), (1,)), ((), ())), preferred_element_type=F32)
        m_new = jnp.maximum(m, jnp.max(s, axis=-1, keepdims=True))
        p = jnp.exp(s - m_new)
        alpha = jnp.exp(m - m_new)
        l = alpha * l + jnp.sum(p, axis=-1, keepdims=True)
        acc = alpha * acc + jnp.dot(p.astype(BF16), vc, preferred_element_type=F32)
        return m_new, l, acc

    init = (jnp.full((rows, 1), -jnp.inf, F32), jnp.zeros((rows, 1), F32), jnp.zeros((rows, HEAD_DIM), F32))
    _, l, acc = lax.fori_loop(0, skv // tk, step, init)
    out = acc / l
    for h in range(Q_PER_KV):
        o_ref[0, :, h * HEAD_DIM:(h + 1) * HEAD_DIM] = out[h * tq:(h + 1) * tq].astype(o_ref.dtype)


def attention(q, k, v):
    b, s, _ = q.shape
    skv = k.shape[1]
    tq = min(ATT_Q_ROWS, s)
    tk = 768 if skv % 768 == 0 else 256
    assert skv % tk == 0 and s % tq == 0
    gw = Q_PER_KV * HEAD_DIM
    kv_spec = BS((1, skv, HEAD_DIM), lambda bi, h, i: (bi, 0, h))
    return pl.pallas_call(
        functools.partial(_attn_kernel, tk=tk),
        out_shape=SDS((b, s, ATTN_WIDTH), BF16),
        grid=(b, N_KV_HEADS, s // tq),
        in_specs=[BS((1, tq, gw), lambda bi, h, i: (bi, i, h)), kv_spec, kv_spec],
        out_specs=BS((1, tq, gw), lambda bi, h, i: (bi, i, h)),
        compiler_params=_params(("parallel", "parallel", "parallel")),
        name="attention",
    )(q, k, v)


def _conv3_kernel(u_ref, w_ref, b_ref, o_ref):
    x = u_ref[0]
    s = x.shape[0]
    row = lax.broadcasted_iota(I32, x.shape, 0)
    prev = jnp.where(row == 0, 0.0, pltpu.roll(x, 1, 0))
    nxt = jnp.where(row == s - 1, 0.0, pltpu.roll(x, s - 1, 0))
    o_ref[0, 0] = prev * w_ref[0:1, :] + x * w_ref[1:2, :] + nxt * w_ref[2:3, :] + b_ref[...]


def hyena_short_conv(proj, conv_w, conv_b):
    b, s, _ = proj.shape
    cblocks = HYENA_WIDTH // LANES
    return pl.pallas_call(
        _conv3_kernel,
        out_shape=SDS((3, b, s, HYENA_WIDTH), F32),
        grid=(b, 3 * cblocks),
        in_specs=[BS((1, s, LANES), lambda bi, c: (bi, 0, U_OFFSET // LANES + c)),
                  BS((3, LANES), lambda bi, c: (0, c)),
                  BS((1, LANES), lambda bi, c: (0, c))],
        out_specs=BS((1, 1, s, LANES), lambda bi, c: (c // cblocks, bi, 0, c % cblocks)),
        compiler_params=_params(("parallel", "parallel")),
        name="hyena_short_conv",
    )(proj, conv_w, conv_b.reshape(1, -1))


def _filter_kernel(z_ref, w1_ref, b1_ref, w2_ref, b2_ref, w3_ref, fr_ref, dl_ref, k_ref, l1_ref, *, seq_len):
    i = pl.program_id(0)
    rows = z_ref.shape[0]
    z = z_ref[...]
    fr = fr_ref[...]
    a = jnp.sin(fr * (jnp.dot(z, w1_ref[...], precision=HIGHEST, preferred_element_type=F32) + b1_ref[...]))
    a = jnp.sin(fr * (jnp.dot(a, w2_ref[...], precision=HIGHEST, preferred_element_type=F32) + b2_ref[...]))
    h = jnp.dot(a, w3_ref[...], precision=HIGHEST, preferred_element_type=F32)
    decay = jnp.exp(-z[:, 0:1] * dl_ref[...])
    n = i * rows + lax.broadcasted_iota(I32, (rows, 1), 0)
    keep = jnp.where(n == seq_len, 0.0, 1.0)

    @pl.when(i == 0)
    def _():
        l1_ref[...] = jnp.zeros_like(l1_ref)

    for o in range(2):
        ko = h[:, o * HYENA_WIDTH:(o + 1) * HYENA_WIDTH] * decay * keep
        k_ref[o] = ko
        l1_ref[o:o + 1, :] += jnp.sum(jnp.abs(ko), axis=0, keepdims=True)


def hyena_filter(seq_len, f_w1, f_b1, f_w2, f_b2, f_w3, f_freq):
    n2 = 2 * seq_len
    n = jnp.arange(n2, dtype=I32)
    lag = jnp.where(n < seq_len, n, n2 - n).astype(F32)
    t01 = lag / max(seq_len - 1, 1)
    bands = (HYENA_EMB_DIM - 1) // 2
    f = jnp.linspace(1e-4, bands - 1, bands, dtype=F32)
    fw = (2.0 * math.pi * lag / seq_len)[:, None] * f[None, :]
    z = jnp.concatenate([t01[:, None], jnp.cos(fw), -jnp.sin(fw),
                         jnp.zeros((n2, LANES - HYENA_EMB_DIM), F32)], -1)
    w1p = jnp.concatenate([f_w1, jnp.zeros((LANES - HYENA_EMB_DIM, HYENA_FILTER_HIDDEN), F32)], 0)
    max_decay = math.log(HYENA_TARGET) / HYENA_FAST_DECAY_PCT
    min_decay = math.log(HYENA_TARGET) / HYENA_SLOW_DECAY_PCT
    deltas = jnp.abs(jnp.linspace(min_decay, max_decay, HYENA_WIDTH, dtype=F32)).reshape(1, -1)
    rows = min(512, seq_len)
    half = seq_len // rows
    hid = HYENA_FILTER_HIDDEN
    full = lambda shp: BS(shp, lambda i: (0,) * len(shp))
    return pl.pallas_call(
        functools.partial(_filter_kernel, seq_len=seq_len),
        out_shape=[SDS((2, n2, HYENA_WIDTH), F32), SDS((SUBLANES, HYENA_WIDTH), F32)],
        grid=(n2 // rows,),
        in_specs=[BS((rows, LANES), lambda i: (i, 0)), full((LANES, hid)), full((1, hid)), full((hid, hid)),
                  full((1, hid)), BS((hid, 2 * HYENA_WIDTH), lambda i: (0, i // half)), full((1, hid)),
                  full((1, HYENA_WIDTH))],
        out_specs=[BS((2, rows, HYENA_WIDTH), lambda i: (0, i, 0)), full((SUBLANES, HYENA_WIDTH))],
        compiler_params=_params(("arbitrary",)),
        name="hyena_filter",
    )(z, w1p, f_b1.reshape(1, -1), f_w2, f_b2.reshape(1, -1), f_w3, f_freq.reshape(1, -1), deltas)


def _pack(re, im):
    hi = lax.bitcast_convert_type(re.astype(BF16).astype(F32), U32)
    lo = lax.bitcast_convert_type(im.astype(BF16).astype(F32), U32)
    return hi | (lo >> 16)


def _unpack(word):
    re = lax.bitcast_convert_type(word & jnp.uint32(0xFFFF0000), F32)
    im = lax.bitcast_convert_type(word << 16, F32)
    return re, im


def _fft_in_kernel(x_ref, m_ref, o_ref):
    r = FFT_RADIX
    for s in range(FFT_GROUP):
        a = jnp.dot(m_ref[s], x_ref[:, s, :].astype(BF16), preferred_element_type=F32)
        o_ref[:, s, :] = _pack(a[:r], a[r:])


def fft_stage_in(x3, mats):
    r = FFT_RADIX
    c = x3.shape[-1]
    g = FFT_GROUP
    return pl.pallas_call(
        _fft_in_kernel,
        out_shape=SDS((r, r, c), U32),
        grid=(r // g,),
        in_specs=[BS((r, g, c), lambda i: (0, i, 0)), BS((g, 2 * r, r), lambda i: (i, 0, 0))],
        out_specs=BS((r, g, c), lambda i: (0, i, 0)),
        compiler_params=_params(("parallel",)),
        name="fft_stage_in",
    )(x3, mats)


def _fft_mid_kernel(*refs, spectrum_only):
    if spectrum_only:
        a_ref, f_ref, sc_ref, o_ref = refs
    else:
        a_ref, h_ref, f_ref, ft_ref, o_ref = refs
    r = FFT_RADIX
    for s in range(a_ref.shape[0]):
        ar, ai = _unpack(a_ref[s])
        x = jnp.dot(f_ref[...], jnp.concatenate([ar, ai], axis=0).astype(BF16), preferred_element_type=F32)
        xr, xi = x[:r], x[r:]
        if spectrum_only:
            o_ref[s] = _pack(xr * sc_ref[...], xi * sc_ref[...])
        else:
            hr, hi = _unpack(h_ref[s])
            yr = xr * hr - xi * hi
            yi = xr * hi + xi * hr
            bmat = jnp.dot(ft_ref[...], jnp.concatenate([yr, yi], axis=0).astype(BF16),
                           preferred_element_type=F32)
            o_ref[s] = _pack(bmat[:r], bmat[r:])


def fft_stage_mid(a_pk, fmat, fmat_t, h_pk=None, scale=None):
    r = FFT_RADIX
    c = a_pk.shape[-1]
    kb = 4
    blk = BS((kb, r, c), lambda i: (i, 0, 0))
    mat = BS((2 * r, 2 * r), lambda i: (0, 0))
    spectrum_only = h_pk is None
    if spectrum_only:
        in_specs, args = [blk, mat, BS((1, c), lambda i: (0, 0))], (a_pk, fmat, scale)
    else:
        in_specs, args = [blk, blk, mat, mat], (a_pk, h_pk, fmat, fmat_t)
    return pl.pallas_call(
        functools.partial(_fft_mid_kernel, spectrum_only=spectrum_only),
        out_shape=SDS((r, r, c), U32),
        grid=(r // kb,),
        in_specs=in_specs,
        out_specs=blk,
        compiler_params=_params(("parallel",)),
        name="fft_stage_mid",
    )(*args)


def _fft_out_kernel(b_ref, m_ref, v_ref, g_ref, d_ref, o_ref):
    for s in range(FFT_GROUP):
        br, bi = _unpack(b_ref[:, s, :])
        y = jnp.dot(m_ref[s], jnp.concatenate([br, bi], axis=0).astype(BF16), preferred_element_type=F32)
        o_ref[:, s, :] = (g_ref[:, s, :] * (y + v_ref[:, s, :] * d_ref[...])).astype(o_ref.dtype)


def fft_stage_out(b_pk, mats_t, v3, gate3, skip, out_dtype):
    r = FFT_RADIX
    c = b_pk.shape[-1]
    g = FFT_GROUP
    blk = BS((r, g, c), lambda i: (0, i, 0))
    return pl.pallas_call(
        _fft_out_kernel,
        out_shape=SDS((r, r, c), out_dtype),
        grid=(r // g,),
        in_specs=[blk, BS((g, r, 2 * r), lambda i: (i, 0, 0)), blk, blk, BS((1, c), lambda i: (0, 0))],
        out_specs=blk,
        compiler_params=_params(("parallel",)),
        name="fft_stage_out",
    )(b_pk, mats_t, v3, gate3, skip)


def _dft_tables(full_input):
    r = FFT_RADIX
    n = r * r
    t1 = jnp.arange(r, dtype=I32)[:, None, None]
    k2 = jnp.arange(r, dtype=I32)[None, :, None]
    nt2 = r if full_input else r // 2
    t2 = jnp.arange(nt2, dtype=I32)[None, None, :]
    ang = (2.0 * math.pi / n) * (((r * t2 + t1) * k2) % n).astype(F32)
    c, s = jnp.cos(ang), jnp.sin(ang)
    if full_input:
        return jnp.concatenate([c, -s], axis=1).astype(BF16)
    top = jnp.concatenate([c, s], axis=2)
    bot = jnp.concatenate([-s, c], axis=2)
    mats = jnp.concatenate([top, bot], axis=1)
    mats_t = jnp.swapaxes(mats, 1, 2) * (1.0 / n)
    return mats.astype(BF16), mats_t.astype(BF16)


def _dft_radix():
    r = FFT_RADIX
    a = jnp.arange(r, dtype=I32)
    ang = (2.0 * math.pi / r) * ((a[:, None] * a[None, :]) % r).astype(F32)
    c, s = jnp.cos(ang), jnp.sin(ang)
    f = jnp.concatenate([jnp.concatenate([c, s], 1), jnp.concatenate([-s, c], 1)], 0)
    return f.astype(BF16), f.T.astype(BF16)


def _dense_dft(n, n_in, full_input):
    k = jnp.arange(n, dtype=I32)[:, None]
    t = jnp.arange(n_in, dtype=I32)[None, :]
    ang = (2.0 * math.pi / n) * ((k * t) % n).astype(F32)
    c, s = jnp.cos(ang), jnp.sin(ang)
    if full_input:
        return jnp.concatenate([c, -s], 0).astype(BF16)
    fwd = jnp.concatenate([jnp.concatenate([c, s], 1), jnp.concatenate([-s, c], 1)], 0)
    return fwd.astype(BF16), (fwd.T * (1.0 / n)).astype(BF16)


def hyena_long(vx, filt, l1, skip):
    _, b, seq, c = vx.shape
    assert b == 2
    r = FFT_RADIX
    inv_l1 = 1.0 / l1[:2]
    if 2 * seq == r * r:
        mats, mats_t = _dft_tables(False)
        mats_f = _dft_tables(True)
        fmat, fmat_t = _dft_radix()
        v3, x1, x2 = (vx[i].reshape(r, r, c) for i in range(3))
        spec = [fft_stage_mid(fft_stage_in(filt[o].reshape(r, r, c), mats_f), fmat, fmat_t,
                              scale=inv_l1[o:o + 1]) for o in range(2)]
        z3 = fft_stage_out(fft_stage_mid(fft_stage_in(v3, mats), fmat, fmat_t, h_pk=spec[0]),
                           mats_t, v3, x1, skip[0:1], F32)
        y3 = fft_stage_out(fft_stage_mid(fft_stage_in(z3, mats), fmat, fmat_t, h_pk=spec[1]),
                           mats_t, z3, x2, skip[1:2], BF16)
        return y3.reshape(b, seq, c)
    n = 2 * seq
    fwd, inv = _dense_dft(n, seq, False)
    fwd_f = _dense_dft(n, n, True)
    v, x1, x2 = (vx[i].reshape(b * seq, c) for i in range(3))

    def conv(u, o):
        hs = matmul([fwd_f], filt[o][None], 0, n_out=c) * inv_l1[o:o + 1]
        xs = matmul([fwd], u[None], 0, n_out=c)
        hr, hi, xr, xi = hs[:n], hs[n:], xs[:n], xs[n:]
        ys = jnp.concatenate([xr * hr - xi * hi, xr * hi + xi * hr], 0)
        return matmul([inv], ys[None], 0, n_out=c)

    z = x1 * (conv(v, 0) + v * skip[0:1])
    y = x2 * (conv(z, 1) + z * skip[1:2])
    return y.astype(BF16).reshape(b, seq, c)


def _dwconv_kernel(x_ref, p_ref, n_ref, w_ref, b_ref, g_ref, be_ref, o_ref, win_ref, y_ref):
    i = pl.program_id(1)
    nt = pl.num_programs(1)
    ts = x_ref.shape[1]
    d = x_ref.shape[2]
    win_ref[0:HALO, :] = jnp.where(i == 0, 0.0, p_ref[0])
    win_ref[HALO:HALO + ts, :] = x_ref[0]
    win_ref[HALO + ts:, :] = jnp.where(i == nt - 1, 0.0, n_ref[0])
    pad = (CONF_K - 1) // 2
    for cb in range(d // LANES):
        cs = slice(cb * LANES, (cb + 1) * LANES)
        acc = jnp.zeros((ts, LANES), F32)
        for j in range(CONF_K):
            acc = acc + win_ref[pl.ds(HALO - pad + j, ts), cs] * w_ref[j:j + 1, cs]
        y_ref[:, cs] = acc + b_ref[:, cs]
    y = y_ref[...]
    mu = jnp.mean(y, axis=-1, keepdims=True)
    yc = y - mu
    var = jnp.mean(yc * yc, axis=-1, keepdims=True)
    a = yc * lax.rsqrt(var + NORM_EPS) * g_ref[...] + be_ref[...]
    o_ref[0] = (a * jax.nn.sigmoid(a)).astype(o_ref.dtype)


def conformer_dwconv(x, w, b, g, be):
    bsz, s, d = x.shape
    ts = LN_ROWS
    hb = ts // HALO
    nhb = s // HALO
    vec = lambda r: BS((r, d), lambda bi, i: (0, 0))
    return pl.pallas_call(
        _dwconv_kernel,
        out_shape=SDS((bsz, s, d), BF16),
        grid=(bsz, s // ts),
        in_specs=[BS((1, ts, d), lambda bi, i: (bi, i, 0)),
                  BS((1, HALO, d), lambda bi, i: (bi, jnp.maximum(i * hb - 1, 0), 0)),
                  BS((1, HALO, d), lambda bi, i: (bi, jnp.minimum((i + 1) * hb, nhb - 1), 0)),
                  vec(CONF_K), vec(1), vec(1), vec(1)],
        out_specs=BS((1, ts, d), lambda bi, i: (bi, i, 0)),
        scratch_shapes=[pltpu.VMEM((ts + 2 * HALO, d), F32), pltpu.VMEM((ts, d), F32)],
        compiler_params=_params(("parallel", "parallel")),
        name="conformer_dwconv",
    )(x, x, x, w, b.reshape(1, d), g.reshape(1, d), be.reshape(1, d))


def _layer_norm(v, g, b):
    mu = jnp.mean(v, axis=-1, keepdims=True)
    vc = v - mu
    var = jnp.mean(vc * vc, axis=-1, keepdims=True)
    return vc * lax.rsqrt(var + NORM_EPS) * g + b


def _route(tok, rw_ref, rb_ref):
    rows = tok.shape[0]
    logits = jnp.dot(tok, rw_ref[...], precision=HIGHEST, preferred_element_type=F32)
    aff = jax.nn.sigmoid(logits)
    sel = aff + rb_ref[...]
    sc = [sel[:, e:e + 1] for e in range(N_EXPERTS)]
    ac = [aff[:, e:e + 1] for e in range(N_EXPERTS)]
    scores = []
    for gi in range(N_GROUPS):
        m = sc[gi * 4:(gi + 1) * 4]
        best = m[0] + m[1]
        for a in range(4):
            for bb in range(a + 1, 4):
                if (a, bb) != (0, 1):
                    best = jnp.maximum(best, m[a] + m[bb])
        scores.append(best)
    top = scores[0]
    grp = jnp.zeros((rows, 1), I32)
    for gi in range(1, N_GROUPS):
        upd = scores[gi] > top
        grp = jnp.where(upd, gi, grp)
        top = jnp.where(upd, scores[gi], top)

    def pick(cols, k):
        out = cols[k]
        for gi in range(1, N_GROUPS):
            out = jnp.where(grp == gi, cols[gi * 4 + k], out)
        return out

    ms = [pick(sc, k) for k in range(4)]
    ma = [pick(ac, k) for k in range(4)]

    def argmax4(vals):
        bv, bi, ba = vals[0], jnp.zeros((rows, 1), I32), ma[0]
        for k in range(1, 4):
            upd = vals[k] > bv
            bi = jnp.where(upd, k, bi)
            ba = jnp.where(upd, ma[k], ba)
            bv = jnp.where(upd, vals[k], bv)
        return bi, ba

    i1, a1 = argmax4(ms)
    i2, a2 = argmax4([jnp.where(i1 == k, -jnp.inf, ms[k]) for k in range(4)])
    tot = a1 + a2
    return grp * 4 + i1, grp * 4 + i2, a1 / tot, a2 / tot


def _ln_router_kernel(x_ref, y_ref, g_ref, lg_ref, lb_ref, sc_ref, sh_ref, rw_ref, rb_ref,
                      xo_ref, tok_ref, rt_ref):
    v = DEEPNORM_ALPHA * x_ref[0] + g_ref[0] * y_ref[0].astype(F32)
    xn = _layer_norm(v, lg_ref[...], lb_ref[...])
    xo_ref[0] = xn
    tok = xn * (1.0 + sc_ref[0]) + sh_ref[0]
    tok_ref[0] = tok
    e1, e2, g1, g2 = _route(tok, rw_ref, rb_ref)
    lane = lax.broadcasted_iota(I32, rt_ref.shape[1:], 1)
    rt_ref[0] = jnp.where(lane == 0, e1.astype(F32),
                          jnp.where(lane == 1, e2.astype(F32),
                                    jnp.where(lane == 2, g1, jnp.where(lane == 3, g2, 0.0))))


def ln_router(x, y, gate, ln_g, ln_b, sc, sh, rw_pad, rb_pad):
    b, s, d = x.shape
    ts = min(LN_ROWS, s)
    tile = BS((1, ts, d), lambda bi, i: (bi, i, 0))
    vec = BS((1, 1, d), lambda bi, i: (bi, 0, 0))
    par = BS((1, d), lambda bi, i: (0, 0))
    return pl.pallas_call(
        _ln_router_kernel,
        out_shape=[SDS((b, s, d), F32), SDS((b, s, d), F32), SDS((b, s, LANES), F32)],
        grid=(b, s // ts),
        in_specs=[tile, tile, vec, par, par, vec, vec,
                  BS((d, LANES), lambda bi, i: (0, 0)), BS((1, LANES), lambda bi, i: (0, 0))],
        out_specs=[tile, tile, BS((1, ts, LANES), lambda bi, i: (bi, i, 0))],
        compiler_params=_params(("parallel", "parallel")),
        name="ln_router",
    )(x, y.reshape(b, s, d), gate, ln_g.reshape(1, d), ln_b.reshape(1, d), sc, sh, rw_pad, rb_pad)


def _row_copy_wait(src_rows_ref, dst_any, sem, n_rows):
    pltpu.make_async_copy(src_rows_ref, dst_any.at[pl.ds(0, n_rows), :], sem).wait()


def _dispatch_kernel(pos_ref, lt_ref, na_ref, lat_ref, ctx_ref, xs_hbm, zero_ref, sem, zsem, *, n_lat, n_tok):
    i = pl.program_id(0)
    ts = lat_ref.shape[0]
    n_tiles = xs_hbm.shape[0] // GMM_ROWS

    def zero_tile(t):
        start = pl.multiple_of(t * GMM_ROWS, GMM_ROWS)
        return pltpu.make_async_copy(zero_ref, xs_hbm.at[pl.ds(start, GMM_ROWS), :], zsem)

    @pl.when(i == 0)
    def _():
        zero_ref[...] = jnp.zeros_like(zero_ref)
        for e in range(N_EXPERTS):
            @pl.when(lt_ref[e] >= 0)
            def _():
                zero_tile(lt_ref[e]).start()

        def start_tail(t, c):
            zero_tile(t).start()
            return c

        lax.fori_loop(na_ref[0], n_tiles, start_tail, 0)
        for e in range(N_EXPERTS):
            @pl.when(lt_ref[e] >= 0)
            def _():
                zero_tile(0).wait()

        def wait_tail(t, c):
            zero_tile(0).wait()
            return c

        lax.fori_loop(na_ref[0], n_tiles, wait_tail, 0)

    def scatter(src_ref):
        base = i * ts

        def body(r, c):
            for k in range(2):
                dst = pos_ref[k * n_tok + base + r]
                pltpu.make_async_copy(src_ref.at[pl.ds(r, 1), :], xs_hbm.at[pl.ds(dst, 1), :], sem).start()
            return c

        lax.fori_loop(0, ts, body, 0)
        for _ in range(2):
            _row_copy_wait(src_ref, xs_hbm, sem, ts)

    if ctx_ref is None:
        scatter(lat_ref)
    else:
        @pl.when(i < n_lat)
        def _():
            scatter(lat_ref)

        @pl.when(i >= n_lat)
        def _():
            scatter(ctx_ref)


def dispatch(pos, last_tile, n_active, tok_lat, tok_ctx, n_rows):
    t_lat, d = tok_lat.shape
    ts = LN_ROWS
    n_lat = t_lat // ts
    n_ctx = 0 if tok_ctx is None else tok_ctx.shape[0] // ts
    n_tok = t_lat + (0 if tok_ctx is None else tok_ctx.shape[0])
    in_specs = [BS((ts, d), lambda i, p, l, na: (jnp.minimum(i, n_lat - 1), 0))]
    args = [tok_lat]
    if tok_ctx is not None:
        in_specs.append(BS((ts, d), lambda i, p, l, na: (jnp.maximum(i - n_lat, 0), 0)))
        args.append(tok_ctx)

    def kern(pos_ref, lt_ref, na_ref, *refs):
        if tok_ctx is None:
            lat_ref, xs_hbm, zero_ref, sem, zsem = refs
            ctx_ref = None
        else:
            lat_ref, ctx_ref, xs_hbm, zero_ref, sem, zsem = refs
        _dispatch_kernel(pos_ref, lt_ref, na_ref, lat_ref, ctx_ref, xs_hbm, zero_ref, sem, zsem,
                         n_lat=n_lat, n_tok=n_tok)

    return pl.pallas_call(
        kern,
        out_shape=SDS((n_rows, d), F32),
        grid_spec=pltpu.PrefetchScalarGridSpec(
            num_scalar_prefetch=3, grid=(n_lat + n_ctx,),
            in_specs=in_specs,
            out_specs=BS(memory_space=pl.ANY),
            scratch_shapes=[pltpu.VMEM((GMM_ROWS, d), F32), pltpu.SemaphoreType.DMA(()),
                            pltpu.SemaphoreType.DMA(())]),
        compiler_params=_params(("arbitrary",)),
        name="dispatch",
    )(pos, last_tile, n_active, *args)


def _gmm_kernel(te_ref, na_ref, x_ref, w1_ref, w3_ref, w2_ref, o_ref):
    i = pl.program_id(0)
    j = pl.program_id(1)

    @pl.when(i < na_ref[0])
    def _():
        x = x_ref[...].astype(BF16)
        a = jnp.dot(x, w1_ref[0, 0].astype(BF16), preferred_element_type=F32)
        b = jnp.dot(x, w3_ref[0, 0].astype(BF16), preferred_element_type=F32)
        h = (a * jax.nn.sigmoid(a) * b).astype(BF16)
        part = jnp.dot(h, w2_ref[0, 0].astype(BF16), preferred_element_type=F32)

        @pl.when(j == 0)
        def _():
            o_ref[...] = part

        @pl.when(j != 0)
        def _():
            o_ref[...] += part

    @pl.when(jnp.logical_and(i >= na_ref[0], j == 0))
    def _():
        o_ref[...] = jnp.zeros_like(o_ref)


def grouped_mlp(tile_expert, n_active, xs, w1, w3, w2, layer):
    n_rows, d = xs.shape
    n_tiles = n_rows // GMM_ROWS

    def tile(i, te, na):
        return jnp.minimum(i, na[0] - 1)

    def half(i, j, na):
        last = na[0] - 1
        return jnp.where(i <= last, jnp.where(i % 2 == 0, j, 1 - j), jnp.where(last % 2 == 0, 1, 0))

    def expert(i, te, na):
        return te[tile(i, te, na)]

    return pl.pallas_call(
        _gmm_kernel,
        out_shape=SDS((n_rows, d), F32),
        grid_spec=pltpu.PrefetchScalarGridSpec(
            num_scalar_prefetch=2, grid=(n_tiles, 2),
            in_specs=[BS((GMM_ROWS, d), lambda i, j, te, na: (tile(i, te, na), 0)),
                      BS((1, 1, d, GMM_HALF), lambda i, j, te, na: (layer, expert(i, te, na), 0, half(i, j, na))),
                      BS((1, 1, d, GMM_HALF), lambda i, j, te, na: (layer, expert(i, te, na), 0, half(i, j, na))),
                      BS((1, 1, GMM_HALF, d), lambda i, j, te, na: (layer, expert(i, te, na), half(i, j, na), 0))],
            out_specs=BS((GMM_ROWS, d), lambda i, j, te, na: (i, 0))),
        compiler_params=_params(("arbitrary", "arbitrary")),
        name="grouped_mlp",
    )(tile_expert, n_active, xs, w1, w3, w2)


def _combine_ln_kernel(pos_ref, x_ref, rt_ref, g_ref, lg_ref, lb_ref, sc_ref, sh_ref, ys_hbm,
                       xo_ref, h_ref, buf_ref, sem, *, tok_off, n_tok, tiles_per_batch, emit_h):
    ts = x_ref.shape[1]
    base = tok_off + (pl.program_id(0) * tiles_per_batch + pl.program_id(1)) * ts

    def body(r, c):
        for k in range(2):
            src = pos_ref[k * n_tok + base + r]
            pltpu.make_async_copy(ys_hbm.at[pl.ds(src, 1), :], buf_ref.at[k, pl.ds(r, 1), :], sem).start()
        return c

    lax.fori_loop(0, ts, body, 0)
    for k in range(2):
        pltpu.make_async_copy(ys_hbm.at[pl.ds(0, ts), :], buf_ref.at[k], sem).wait()
    rt = rt_ref[0]
    moe = rt[:, 2:3] * buf_ref[0] + rt[:, 3:4] * buf_ref[1]
    xn = _layer_norm(DEEPNORM_ALPHA * x_ref[0] + g_ref[0] * moe, lg_ref[...], lb_ref[...])
    xo_ref[0] = xn
    if emit_h:
        h_ref[0] = (xn * (1.0 + sc_ref[0]) + sh_ref[0]).astype(h_ref.dtype)
    else:
        h_ref[0] = jnp.zeros(h_ref.shape[1:], h_ref.dtype)


def combine_ln(pos, x, route, gate, ln_g, ln_b, sc, sh, ys, *, tok_off, n_tok, emit_h=True):
    b, s, d = x.shape
    ts = min(LN_ROWS, s)
    tpb = s // ts
    tile = BS((1, ts, d), lambda bi, i, p: (bi, i, 0))
    vec = BS((1, 1, d), lambda bi, i, p: (bi, 0, 0))
    par = BS((1, d), lambda bi, i, p: (0, 0))
    h_rows = ts if emit_h else SUBLANES
    return pl.pallas_call(
        functools.partial(_combine_ln_kernel, tok_off=tok_off, n_tok=n_tok, tiles_per_batch=tpb, emit_h=emit_h),
        out_shape=[SDS((b, s, d), F32), SDS((b, tpb * h_rows, d), BF16)],
        grid_spec=pltpu.PrefetchScalarGridSpec(
            num_scalar_prefetch=1, grid=(b, tpb),
            in_specs=[tile, BS((1, ts, LANES), lambda bi, i, p: (bi, i, 0)), vec, par, par, vec, vec,
                      BS(memory_space=pl.ANY)],
            out_specs=[tile, BS((1, h_rows, d), lambda bi, i, p: (bi, i, 0))],
            scratch_shapes=[pltpu.VMEM((2, ts, d), F32), pltpu.SemaphoreType.DMA(())]),
        compiler_params=_params(("arbitrary", "arbitrary")),
        name="combine_ln",
    )(pos, x, route, gate, ln_g.reshape(1, d), ln_b.reshape(1, d), sc, sh, ys)


def _rope_tables(n_tokens):
    rows = n_tokens // GRID_W
    row = jnp.repeat(jnp.arange(rows, dtype=F32), GRID_W)
    col = jnp.tile(jnp.arange(GRID_W, dtype=F32), rows)
    half = HEAD_DIM // 2
    inv = ROPE_THETA ** (-jnp.arange(0, half, 2, dtype=F32) / half)
    ang = jnp.concatenate([row[:, None] * inv, col[:, None] * inv], -1)
    cos, sin = jnp.cos(ang), jnp.sin(ang)
    return jnp.concatenate([cos, cos], -1), jnp.concatenate([-sin, sin], -1)


def _head_perm():
    base = jnp.concatenate([jnp.arange(0, HEAD_DIM, 2), jnp.arange(1, HEAD_DIM, 2)])
    n_heads = N_Q_HEADS + N_KV_HEADS
    return base, jnp.concatenate([h * HEAD_DIM + base for h in range(n_heads)])


def _even_mixer(h, hc, ctx_out, cos_t, sin_t, w_in, q_g, k_g, conv_w, conv_b,
                f_w1, f_b1, f_w2, f_b2, f_w3, f_freq, skip, w_out_all, li):
    b, s, d = h.shape
    lc = hc.shape[1]
    base, perm = _head_perm()
    n_perm = perm.shape[0]
    w_in_p = jnp.concatenate([w_in[:, :n_perm][:, perm], w_in[:, n_perm:]], axis=1)[None]
    qg = q_g[base].reshape(1, HEAD_DIM)
    kg = k_g[base].reshape(1, HEAD_DIM)
    n_in = w_in.shape[1]

    proj = matmul([h.reshape(b * s, d)], w_in_p, 0, n_out=n_in, tn=768).reshape(b, s, n_in)
    proj_c = matmul([hc.reshape(b * lc, d)], w_in_p, 0, n_out=n_in, tn=768).reshape(b, lc, n_in)
    q, k, v = qkv_prepare(proj, qg, kg, cos_t, sin_t, rope=True, with_q=True)
    qc, kc, vc = qkv_prepare(proj_c, qg, kg, None, None, rope=False, with_q=ctx_out)
    att = attention(q, jnp.concatenate([k, kc], axis=1), jnp.concatenate([v, vc], axis=1))

    def hyena(p, seq):
        vx = hyena_short_conv(p, conv_w, conv_b)
        filt, l1 = hyena_filter(seq, f_w1, f_b1, f_w2, f_b2, f_w3, f_freq)
        return hyena_long(vx, filt, l1, skip)

    hy = hyena(proj, s)
    y = matmul([att.reshape(b * s, ATTN_WIDTH), hy.reshape(b * s, HYENA_WIDTH)], w_out_all, li,
               n_out=d).reshape(b, s, d)
    if not ctx_out:
        return y, None
    att_c = attention(qc, kc, vc)
    hy_c = hyena(proj_c, lc)
    yc = matmul([att_c.reshape(b * lc, ATTN_WIDTH), hy_c.reshape(b * lc, HYENA_WIDTH)], w_out_all, li,
                n_out=d).reshape(b, lc, d)
    return y, yc


def _conformer(h, in_w, in_b, dw_w, dw_b, n_g, n_b, out_w, out_b, li):
    b, s, d = h.shape
    a = matmul([h.reshape(b * s, d)], in_w, li, n_out=d, bias=in_b, glu=True).reshape(b, s, d)
    a = conformer_dwconv(a, dw_w[li], dw_b[li], n_g[li], n_b[li])
    return matmul([a.reshape(b * s, d)], out_w, li, n_out=d, bias=out_b).reshape(b, s, d)


def _dispatch_plan(experts, n_tiles):
    n_tok = experts.shape[0]
    flat = experts.T.reshape(-1)
    onehot = (flat[:, None] == jnp.arange(N_EXPERTS, dtype=I32)[None, :]).astype(I32)
    csum = jnp.cumsum(onehot, axis=0)
    rank = jnp.take_along_axis(csum - onehot, flat[:, None], axis=1)[:, 0]
    counts = csum[-1]
    padded = ((counts + GMM_ROWS - 1) // GMM_ROWS) * GMM_ROWS
    ends = jnp.cumsum(padded)
    starts = ends - padded
    pos = (starts[flat] + rank).astype(I32)
    n_active = (ends[-1] // GMM_ROWS).astype(I32).reshape(1)
    tile_start = jnp.arange(n_tiles, dtype=I32) * GMM_ROWS
    tile_expert = jnp.minimum(jnp.sum((ends[None, :] <= tile_start[:, None]).astype(I32), axis=1),
                              N_EXPERTS - 1).astype(I32)
    last_tile = jnp.where(padded > 0, ends // GMM_ROWS - 1, -1).astype(I32)
    del n_tok
    return pos, n_active, tile_expert, last_tile


def kernel(x, c, ctx, c_ctx, mod_w, mod_b, ln_g, ln_b, mix_in_w, q_norm_g, k_norm_g, hy_conv_w, hy_conv_b,
           hy_w1, hy_b1, hy_w2, hy_b2, hy_w3, hy_freq, hy_skip, mix_out_w, cf_in_w, cf_in_b, cf_dw_w, cf_dw_b,
           cf_ln_g, cf_ln_b, cf_out_w, cf_out_b, router_w, router_b, exp_w1, exp_w3, exp_w2):
    b, s, d = x.shape
    lc = ctx.shape[1]
    cos_t, sin_t = _rope_tables(s)
    cvec = jnp.concatenate([c, c_ctx[None], jnp.zeros((SUBLANES - b - 1, d), F32)], 0)
    mods = mod_vectors(cvec, mod_w, mod_b)
    rw_pad = jnp.concatenate([router_w, jnp.zeros((d, LANES - N_EXPERTS), F32)], 1)
    rb_pad = jnp.concatenate([router_b, jnp.zeros((LANES - N_EXPERTS,), F32)]).reshape(1, LANES)

    def mod_of(l, j, is_ctx):
        m = mods[l, :, j * d:(j + 1) * d]
        rows = jnp.broadcast_to(m[b:b + 1], (b, d)) if is_ctx else m[:b]
        return rows.reshape(b, 1, d)

    h = modulate(x, mod_of(0, 1, False), mod_of(0, 0, False))
    hc = modulate(ctx, mod_of(0, 1, True), mod_of(0, 0, True))
    for l in range(DEPTH):
        ctx_needed = any(j > l and j % 2 == 0 for j in range(DEPTH))
        li = l // 2
        if l % 2 == 0:
            y, yc = _even_mixer(h, hc, ctx_needed, cos_t, sin_t, mix_in_w[li], q_norm_g[li], k_norm_g[li],
                                hy_conv_w[li], hy_conv_b[li], hy_w1[li], hy_b1[li], hy_w2[li], hy_b2[li],
                                hy_w3[li], hy_freq[li], hy_skip[li], mix_out_w, li)
        else:
            cf = (cf_in_w, cf_in_b, cf_dw_w, cf_dw_b, cf_ln_g, cf_ln_b, cf_out_w, cf_out_b)
            y = _conformer(h, *cf, li)
            yc = _conformer(hc, *cf, li) if ctx_needed else None

        x, tok, route = ln_router(x, y, mod_of(l, 2, False), ln_g[l, 0], ln_b[l, 0],
                                  mod_of(l, 4, False), mod_of(l, 3, False), rw_pad, rb_pad)
        routes = [route.reshape(b * s, LANES)]
        tok_c = None
        if ctx_needed:
            ctx, tok_c, route_c = ln_router(ctx, yc, mod_of(l, 2, True), ln_g[l, 0], ln_b[l, 0],
                                            mod_of(l, 4, True), mod_of(l, 3, True), rw_pad, rb_pad)
            routes.append(route_c.reshape(b * lc, LANES))
            tok_c = tok_c.reshape(b * lc, d)
        n_tok = b * s + (b * lc if ctx_needed else 0)
        n_tiles = (2 * n_tok) // GMM_ROWS + N_EXPERTS
        experts = jnp.concatenate([r[:, :2] for r in routes], 0).astype(I32)
        pos, n_active, tile_expert, last_tile = _dispatch_plan(experts, n_tiles)
        xs = dispatch(pos, last_tile, n_active, tok.reshape(b * s, d), tok_c, n_tiles * GMM_ROWS)
        ys = grouped_mlp(tile_expert, n_active, xs, exp_w1, exp_w3, exp_w2, l)

        last = l == DEPTH - 1
        nl = min(l + 1, DEPTH - 1)
        x, h = combine_ln(pos, x, route, mod_of(l, 5, False), ln_g[l, 1], ln_b[l, 1],
                          mod_of(nl, 1, False), mod_of(nl, 0, False), ys, tok_off=0, n_tok=n_tok,
                          emit_h=not last)
        if ctx_needed:
            ctx, hc = combine_ln(pos, ctx, route_c, mod_of(l, 5, True), ln_g[l, 1], ln_b[l, 1],
                                 mod_of(nl, 1, True), mod_of(nl, 0, True), ys, tok_off=b * s, n_tok=n_tok)
    return x
```

```python
import functools
import math

import jax
import jax.numpy as jnp
from jax import lax
from jax.experimental import pallas as pl
from jax.experimental.pallas import tpu as pltpu

F32 = jnp.float32
BF16 = jnp.bfloat16
U32 = jnp.uint32
I32 = jnp.int32
HIGHEST = lax.Precision.HIGHEST
SDS = jax.ShapeDtypeStruct
BS = pl.BlockSpec

D_MODEL = 2048
DEPTH = 4
GRID_W = 64
HEAD_DIM = 128
N_Q_HEADS = 8
N_KV_HEADS = 2
Q_PER_KV = N_Q_HEADS // N_KV_HEADS
ATTN_WIDTH = N_Q_HEADS * HEAD_DIM
KV_WIDTH = N_KV_HEADS * HEAD_DIM
ATTN_SCALE = HEAD_DIM ** -0.5
ROPE_THETA = 10000.0
HYENA_WIDTH = D_MODEL - ATTN_WIDTH
HYENA_EMB_DIM = 33
HYENA_FILTER_HIDDEN = 64
HYENA_FAST_DECAY_PCT = 0.3
HYENA_SLOW_DECAY_PCT = 1.5
HYENA_TARGET = 1e-2
U_OFFSET = ATTN_WIDTH + 2 * KV_WIDTH
CONF_K = 31
N_EXPERTS = 16
N_GROUPS = 4
EXPERTS_PER_GROUP = 4
D_EXPERT = 1024
DEEPNORM_ALPHA = (2 * DEPTH) ** 0.25
N_MOD = 6
NORM_EPS = 1e-6
LOG2_E = math.log2(math.e)

LANES = 128
SUBLANES = 8
VMEM_LIMIT_BYTES = 56 * 1024 * 1024

LN_ROWS = 256
MM_ROWS = 1024
GMM_ROWS = 512
GMM_HALF = D_EXPERT // 2
FFT_RADIX = 128
FFT_GROUP = 8
ATT_Q_ROWS = 512
ATT_K_ROWS = 768
VT_ROWS = HEAD_DIM + 16
HALO = 16
DW_ROWS = 64
ROW_COPY_UNROLL = 8


def _params(semantics):
    return pltpu.CompilerParams(dimension_semantics=semantics, vmem_limit_bytes=VMEM_LIMIT_BYTES)


def _split(x):
    hi = x.astype(BF16)
    return hi, (x - hi.astype(F32)).astype(BF16)


def _dot3(x, w_hi, w_lo):
    x_hi, x_lo = _split(x)
    dot = functools.partial(jnp.dot, preferred_element_type=F32)
    return dot(x_hi, w_hi) + (dot(x_hi, w_lo) + dot(x_lo, w_hi))


def _mod_kernel(c_ref, w_ref, b_ref, o_ref):
    c = c_ref[...]
    s = c * jax.nn.sigmoid(c)
    o_ref[0] = jnp.dot(s, w_ref[0], precision=HIGHEST, preferred_element_type=F32) + b_ref[0]


def mod_vectors(cvec, mod_w, mod_b):
    depth, d, n = mod_w.shape
    tn = 1024
    return pl.pallas_call(
        _mod_kernel,
        out_shape=SDS((depth, SUBLANES, n), F32),
        grid=(depth, n // tn),
        in_specs=[BS((SUBLANES, d), lambda l, j: (0, 0)),
                  BS((1, d, tn), lambda l, j: (l, 0, j)),
                  BS((1, 1, tn), lambda l, j: (l, 0, j))],
        out_specs=BS((1, SUBLANES, tn), lambda l, j: (l, 0, j)),
        compiler_params=_params(("parallel", "parallel")),
        name="mod_vectors",
    )(cvec, mod_w, mod_b.reshape(depth, 1, n))


def _modulate_kernel(x_ref, sc_ref, sh_ref, o_ref):
    o_ref[0] = (x_ref[0] * (1.0 + sc_ref[0]) + sh_ref[0]).astype(o_ref.dtype)


def modulate(x, sc, sh):
    b, s, d = x.shape
    ts = min(LN_ROWS, s)
    vec = BS((1, 1, d), lambda bi, i: (bi, 0, 0))
    return pl.pallas_call(
        _modulate_kernel,
        out_shape=SDS((b, s, d), BF16),
        grid=(b, s // ts),
        in_specs=[BS((1, ts, d), lambda bi, i: (bi, i, 0)), vec, vec],
        out_specs=BS((1, ts, d), lambda bi, i: (bi, i, 0)),
        compiler_params=_params(("parallel", "parallel")),
        name="modulate",
    )(x, sc, sh)


def _mm_kernel(*refs, k_sizes, has_bias, glu):
    n_a = len(k_sizes)
    a_refs = refs[:n_a]
    pos = n_a
    w_refs = refs[pos:pos + (2 if glu else 1)]
    pos += len(w_refs)
    b_refs = ()
    if has_bias:
        b_refs = refs[pos:pos + len(w_refs)]
        pos += len(w_refs)
    o_ref = refs[pos]
    wbf_refs = refs[pos + 1:]

    @pl.when(pl.program_id(1) == 0)
    def _():
        for w_ref, wbf in zip(w_refs, wbf_refs):
            wbf[...] = w_ref[0].astype(BF16)

    def linear(which):
        acc = None
        off = 0
        for a_ref, k in zip(a_refs, k_sizes):
            part = jnp.dot(a_ref[...].astype(BF16), wbf_refs[which][off:off + k, :],
                           preferred_element_type=F32)
            acc = part if acc is None else acc + part
            off += k
        if has_bias:
            acc = acc + b_refs[which][0]
        return acc

    out = linear(0)
    if glu:
        out = out * jax.nn.sigmoid(linear(1))
    o_ref[...] = out.astype(o_ref.dtype)


def matmul(a_list, w, layer, *, n_out, col_off=0, bias=None, glu=False, out_dtype=F32, tn=512):
    m = a_list[0].shape[0]
    k_sizes = tuple(a.shape[1] for a in a_list)
    k_total = sum(k_sizes)
    assert w.shape[1] == k_total and n_out % tn == 0 and col_off % tn == 0
    tm = min(MM_ROWS, m)
    assert m % tm == 0
    nb = n_out // tn
    cb = col_off // tn
    in_specs = [BS((tm, k), lambda j, i: (i, 0)) for k in k_sizes]
    args = list(a_list)
    n_w = 2 if glu else 1
    for h in range(n_w):
        in_specs.append(BS((1, k_total, tn), lambda j, i, h=h: (layer, 0, cb + h * nb + j)))
        args.append(w)
    if bias is not None:
        bias3 = bias.reshape(bias.shape[0], 1, bias.shape[1])
        for h in range(n_w):
            in_specs.append(BS((1, 1, tn), lambda j, i, h=h: (layer, 0, cb + h * nb + j)))
            args.append(bias3)
    return pl.pallas_call(
        functools.partial(_mm_kernel, k_sizes=k_sizes, has_bias=bias is not None, glu=glu),
        out_shape=SDS((m, n_out), out_dtype),
        grid=(nb, m // tm),
        in_specs=in_specs,
        out_specs=BS((tm, tn), lambda j, i: (i, j)),
        scratch_shapes=[pltpu.VMEM((k_total, tn), BF16) for _ in range(n_w)],
        compiler_params=_params(("parallel", "arbitrary")),
        name="matmul_glu" if glu else "matmul",
    )(*args)


def _qkv_kernel(*refs, rope, n_q_heads, n_k_heads):
    it = iter(refs)
    q_ref = next(it) if n_q_heads else None
    kv_ref = next(it)
    qg_ref = next(it)
    kg_ref = next(it)
    cos_ref = sin_ref = None
    if rope:
        cos_ref = next(it)
        sin_ref = next(it)
    qo_ref = next(it) if n_q_heads else None
    ko_ref = next(it)
    vo_ref = next(it)

    def prep(xh, g, scale):
        ms = jnp.mean(xh * xh, axis=-1, keepdims=True)
        xn = xh * lax.rsqrt(ms + NORM_EPS) * g
        if rope:
            xn = xn * cos_ref[...] + pltpu.roll(xn, HEAD_DIM // 2, 1) * sin_ref[...]
        if scale != 1.0:
            xn = xn * scale
        return xn.astype(BF16)

    for h in range(n_q_heads):
        sl = slice(h * HEAD_DIM, (h + 1) * HEAD_DIM)
        qo_ref[0, :, sl] = prep(q_ref[0, :, sl], qg_ref[...], ATTN_SCALE * LOG2_E)
    for h in range(n_k_heads):
        sl = slice(h * HEAD_DIM, (h + 1) * HEAD_DIM)
        ko_ref[0, :, sl] = prep(kv_ref[0, :, sl], kg_ref[...], 1.0)
    vo_ref[0] = kv_ref[0, :, KV_WIDTH:].astype(BF16)


def qkv_prepare(proj, q_gain, k_gain, cos_t, sin_t, *, rope, with_q):
    b, s, _ = proj.shape
    ts = min(512, s)
    n_q = N_Q_HEADS if with_q else 0
    in_specs, args = [], []
    if with_q:
        in_specs.append(BS((1, ts, ATTN_WIDTH), lambda bi, i: (bi, i, 0)))
        args.append(proj)
    in_specs.append(BS((1, ts, 2 * KV_WIDTH), lambda bi, i: (bi, i, ATTN_WIDTH // (2 * KV_WIDTH))))
    args.append(proj)
    gain = BS((1, HEAD_DIM), lambda bi, i: (0, 0))
    in_specs += [gain, gain]
    args += [q_gain, k_gain]
    if rope:
        tab = BS((ts, HEAD_DIM), lambda bi, i: (i, 0))
        in_specs += [tab, tab]
        args += [cos_t, sin_t]
    out_shape, out_specs = [], []
    if with_q:
        out_shape.append(SDS((b, s, ATTN_WIDTH), BF16))
        out_specs.append(BS((1, ts, ATTN_WIDTH), lambda bi, i: (bi, i, 0)))
    for _ in range(2):
        out_shape.append(SDS((b, s, KV_WIDTH), BF16))
        out_specs.append(BS((1, ts, KV_WIDTH), lambda bi, i: (bi, i, 0)))
    res = pl.pallas_call(
        functools.partial(_qkv_kernel, rope=rope, n_q_heads=n_q, n_k_heads=N_KV_HEADS),
        out_shape=out_shape,
        grid=(b, s // ts),
        in_specs=in_specs,
        out_specs=out_specs,
        compiler_params=_params(("parallel", "parallel")),
        name="qkv_prepare",
    )(*args)
    return res if with_q else (None,) + tuple(res)


def _attn_kernel(q_ref, k_ref, vt_ref, o_ref, s_ref, *, tk):
    tq = q_ref.shape[1]
    skv = k_ref.shape[1]
    cols = Q_PER_KV * tq
    n = skv // tk
    qs = jnp.concatenate([q_ref[0, :, h * HEAD_DIM:(h + 1) * HEAD_DIM] for h in range(Q_PER_KV)], axis=0)

    def scores(c, slot):
        start = pl.multiple_of(c * tk, tk)
        s_ref[slot] = lax.dot_general(k_ref[0, pl.ds(start, tk), :], qs, (((1,), (1,)), ((), ())),
                                      preferred_element_type=F32)

    def consume(c, slot, m, acc):
        start = pl.multiple_of(c * tk, tk)
        vt = vt_ref[0, 0, :, pl.ds(start, tk)]
        m_new = jnp.maximum(m, jnp.max(s_ref[slot], axis=0, keepdims=True))
        p = jnp.exp2(s_ref[slot] - m_new).astype(BF16)
        alpha = jnp.exp2(m - m_new)
        return m_new, alpha * acc + jnp.dot(vt, p, preferred_element_type=F32)

    def pair(i, carry):
        m, acc = carry
        c = 2 * i
        scores(c + 1, 1)
        m, acc = consume(c, 0, m, acc)
        scores(c + 2, 0)
        return consume(c + 1, 1, m, acc)

    scores(0, 0)
    init = (jnp.full((1, cols), -jnp.inf, F32), jnp.zeros((VT_ROWS, cols), F32))
    m, acc = lax.fori_loop(0, n // 2, pair, init)
    _, acc = consume(n - 1, 0, m, acc)
    out = (acc[:HEAD_DIM] / acc[HEAD_DIM:HEAD_DIM + 1]).T
    for h in range(Q_PER_KV):
        o_ref[0, :, h * HEAD_DIM:(h + 1) * HEAD_DIM] = out[h * tq:(h + 1) * tq].astype(o_ref.dtype)


def attention(q, k, v):
    b, s, _ = q.shape
    skv = k.shape[1]
    tq = min(ATT_Q_ROWS, s)
    tk = ATT_K_ROWS if skv % ATT_K_ROWS == 0 else skv
    n = skv // tk
    assert skv % tk == 0 and s % tq == 0 and n % 2 == 1
    gw = Q_PER_KV * HEAD_DIM
    vt = jnp.swapaxes(v, 1, 2).reshape(b, N_KV_HEADS, HEAD_DIM, skv)
    vt = jnp.concatenate([vt, jnp.ones((b, N_KV_HEADS, VT_ROWS - HEAD_DIM, skv), BF16)], axis=2)
    return pl.pallas_call(
        functools.partial(_attn_kernel, tk=tk),
        out_shape=SDS((b, s, ATTN_WIDTH), BF16),
        grid=(b, N_KV_HEADS, s // tq),
        in_specs=[BS((1, tq, gw), lambda bi, h, i: (bi, i, h)),
                  BS((1, skv, HEAD_DIM), lambda bi, h, i: (bi, 0, h)),
                  BS((1, 1, VT_ROWS, skv), lambda bi, h, i: (bi, h, 0, 0))],
        out_specs=BS((1, tq, gw), lambda bi, h, i: (bi, i, h)),
        scratch_shapes=[pltpu.VMEM((2, tk, Q_PER_KV * tq), F32)],
        compiler_params=_params(("parallel", "parallel", "parallel")),
        name="attention",
    )(q, k, vt)


def _conv3_kernel(u_ref, w_ref, b_ref, o_ref):
    x = u_ref[0]
    s = x.shape[0]
    row = lax.broadcasted_iota(I32, x.shape, 0)
    prev = jnp.where(row == 0, 0.0, pltpu.roll(x, 1, 0))
    nxt = jnp.where(row == s - 1, 0.0, pltpu.roll(x, s - 1, 0))
    o_ref[0, 0] = prev * w_ref[0:1, :] + x * w_ref[1:2, :] + nxt * w_ref[2:3, :] + b_ref[...]


def hyena_short_conv(proj, conv_w, conv_b):
    b, s, _ = proj.shape
    cblocks = HYENA_WIDTH // LANES
    return pl.pallas_call(
        _conv3_kernel,
        out_shape=SDS((3, b, s, HYENA_WIDTH), F32),
        grid=(b, 3 * cblocks),
        in_specs=[BS((1, s, LANES), lambda bi, c: (bi, 0, U_OFFSET // LANES + c)),
                  BS((3, LANES), lambda bi, c: (0, c)),
                  BS((1, LANES), lambda bi, c: (0, c))],
        out_specs=BS((1, 1, s, LANES), lambda bi, c: (c // cblocks, bi, 0, c % cblocks)),
        compiler_params=_params(("parallel", "parallel")),
        name="hyena_short_conv",
    )(proj, conv_w, conv_b.reshape(1, -1))


def _filter_kernel(z_ref, w1_ref, b1_ref, w2_ref, b2_ref, w3_ref, fr_ref, dl_ref, k_ref, l1_ref, *, seq_len):
    i = pl.program_id(0)
    rows = z_ref.shape[0]
    z = z_ref[...]
    fr = fr_ref[...]
    a = jnp.sin(fr * (jnp.dot(z, w1_ref[...], precision=HIGHEST, preferred_element_type=F32) + b1_ref[...]))
    a = jnp.sin(fr * (jnp.dot(a, w2_ref[...], precision=HIGHEST, preferred_element_type=F32) + b2_ref[...]))
    h = jnp.dot(a, w3_ref[...], precision=HIGHEST, preferred_element_type=F32)
    decay = jnp.exp(-z[:, 0:1] * dl_ref[...])
    n = i * rows + lax.broadcasted_iota(I32, (rows, 1), 0)
    keep = jnp.where(n == seq_len, 0.0, 1.0)

    @pl.when(i == 0)
    def _():
        l1_ref[...] = jnp.zeros_like(l1_ref)

    for o in range(2):
        ko = h[:, o * HYENA_WIDTH:(o + 1) * HYENA_WIDTH] * decay * keep
        k_ref[o] = ko
        l1_ref[o:o + 1, :] += jnp.sum(jnp.abs(ko), axis=0, keepdims=True)


def hyena_filter(seq_len, f_w1, f_b1, f_w2, f_b2, f_w3, f_freq):
    n2 = 2 * seq_len
    n = jnp.arange(n2, dtype=I32)
    lag = jnp.where(n < seq_len, n, n2 - n).astype(F32)
    t01 = lag / max(seq_len - 1, 1)
    bands = (HYENA_EMB_DIM - 1) // 2
    f = jnp.linspace(1e-4, bands - 1, bands, dtype=F32)
    fw = (2.0 * math.pi * lag / seq_len)[:, None] * f[None, :]
    z = jnp.concatenate([t01[:, None], jnp.cos(fw), -jnp.sin(fw),
                         jnp.zeros((n2, LANES - HYENA_EMB_DIM), F32)], -1)
    w1p = jnp.concatenate([f_w1, jnp.zeros((LANES - HYENA_EMB_DIM, HYENA_FILTER_HIDDEN), F32)], 0)
    max_decay = math.log(HYENA_TARGET) / HYENA_FAST_DECAY_PCT
    min_decay = math.log(HYENA_TARGET) / HYENA_SLOW_DECAY_PCT
    deltas = jnp.abs(jnp.linspace(min_decay, max_decay, HYENA_WIDTH, dtype=F32)).reshape(1, -1)
    rows = min(512, seq_len)
    half = seq_len // rows
    hid = HYENA_FILTER_HIDDEN
    full = lambda shp: BS(shp, lambda i: (0,) * len(shp))
    return pl.pallas_call(
        functools.partial(_filter_kernel, seq_len=seq_len),
        out_shape=[SDS((2, n2, HYENA_WIDTH), F32), SDS((SUBLANES, HYENA_WIDTH), F32)],
        grid=(n2 // rows,),
        in_specs=[BS((rows, LANES), lambda i: (i, 0)), full((LANES, hid)), full((1, hid)), full((hid, hid)),
                  full((1, hid)), BS((hid, 2 * HYENA_WIDTH), lambda i: (0, i // half)), full((1, hid)),
                  full((1, HYENA_WIDTH))],
        out_specs=[BS((2, rows, HYENA_WIDTH), lambda i: (0, i, 0)), full((SUBLANES, HYENA_WIDTH))],
        compiler_params=_params(("arbitrary",)),
        name="hyena_filter",
    )(z, w1p, f_b1.reshape(1, -1), f_w2, f_b2.reshape(1, -1), f_w3, f_freq.reshape(1, -1), deltas)


def _pack(re, im):
    hi = lax.bitcast_convert_type(re.astype(BF16).astype(F32), U32)
    lo = lax.bitcast_convert_type(im.astype(BF16).astype(F32), U32)
    return hi | (lo >> 16)


def _unpack(word):
    re = lax.bitcast_convert_type(word & jnp.uint32(0xFFFF0000), F32)
    im = lax.bitcast_convert_type(word << 16, F32)
    return re, im


def _fft_in_kernel(x_ref, m_ref, o_ref):
    r = FFT_RADIX
    for s in range(FFT_GROUP):
        a = jnp.dot(m_ref[s], x_ref[:, s, :].astype(BF16), preferred_element_type=F32)
        o_ref[:, s, :] = _pack(a[:r], a[r:])


def fft_stage_in(x3, mats):
    r = FFT_RADIX
    c = x3.shape[-1]
    g = FFT_GROUP
    return pl.pallas_call(
        _fft_in_kernel,
        out_shape=SDS((r, r, c), U32),
        grid=(r // g,),
        in_specs=[BS((r, g, c), lambda i: (0, i, 0)), BS((g, 2 * r, r), lambda i: (i, 0, 0))],
        out_specs=BS((r, g, c), lambda i: (0, i, 0)),
        compiler_params=_params(("parallel",)),
        name="fft_stage_in",
    )(x3, mats)


def _fft_mid_kernel(*refs, spectrum_only):
    if spectrum_only:
        a_ref, f_ref, sc_ref, o_ref = refs
    else:
        a_ref, h_ref, f_ref, ft_ref, o_ref = refs
    r = FFT_RADIX
    for s in range(a_ref.shape[0]):
        ar, ai = _unpack(a_ref[s])
        x = jnp.dot(f_ref[...], jnp.concatenate([ar, ai], axis=0).astype(BF16), preferred_element_type=F32)
        xr, xi = x[:r], x[r:]
        if spectrum_only:
            o_ref[s] = _pack(xr * sc_ref[...], xi * sc_ref[...])
        else:
            hr, hi = _unpack(h_ref[s])
            yr = xr * hr - xi * hi
            yi = xr * hi + xi * hr
            bmat = jnp.dot(ft_ref[...], jnp.concatenate([yr, yi], axis=0).astype(BF16),
                           preferred_element_type=F32)
            o_ref[s] = _pack(bmat[:r], bmat[r:])


def fft_stage_mid(a_pk, fmat, fmat_t, h_pk=None, scale=None):
    r = FFT_RADIX
    c = a_pk.shape[-1]
    kb = 4
    blk = BS((kb, r, c), lambda i: (i, 0, 0))
    mat = BS((2 * r, 2 * r), lambda i: (0, 0))
    spectrum_only = h_pk is None
    if spectrum_only:
        in_specs, args = [blk, mat, BS((1, c), lambda i: (0, 0))], (a_pk, fmat, scale)
    else:
        in_specs, args = [blk, blk, mat, mat], (a_pk, h_pk, fmat, fmat_t)
    return pl.pallas_call(
        functools.partial(_fft_mid_kernel, spectrum_only=spectrum_only),
        out_shape=SDS((r, r, c), U32),
        grid=(r // kb,),
        in_specs=in_specs,
        out_specs=blk,
        compiler_params=_params(("parallel",)),
        name="fft_stage_mid",
    )(*args)


def _fft_out_kernel(b_ref, m_ref, v_ref, g_ref, d_ref, o_ref):
    for s in range(FFT_GROUP):
        br, bi = _unpack(b_ref[:, s, :])
        y = jnp.dot(m_ref[s], jnp.concatenate([br, bi], axis=0).astype(BF16), preferred_element_type=F32)
        o_ref[:, s, :] = (g_ref[:, s, :] * (y + v_ref[:, s, :] * d_ref[...])).astype(o_ref.dtype)


def fft_stage_out(b_pk, mats_t, v3, gate3, skip, out_dtype):
    r = FFT_RADIX
    c = b_pk.shape[-1]
    g = FFT_GROUP
    blk = BS((r, g, c), lambda i: (0, i, 0))
    return pl.pallas_call(
        _fft_out_kernel,
        out_shape=SDS((r, r, c), out_dtype),
        grid=(r // g,),
        in_specs=[blk, BS((g, r, 2 * r), lambda i: (i, 0, 0)), blk, blk, BS((1, c), lambda i: (0, 0))],
        out_specs=blk,
        compiler_params=_params(("parallel",)),
        name="fft_stage_out",
    )(b_pk, mats_t, v3, gate3, skip)


def _dft_tables(full_input):
    r = FFT_RADIX
    n = r * r
    t1 = jnp.arange(r, dtype=I32)[:, None, None]
    k2 = jnp.arange(r, dtype=I32)[None, :, None]
    nt2 = r if full_input else r // 2
    t2 = jnp.arange(nt2, dtype=I32)[None, None, :]
    ang = (2.0 * math.pi / n) * (((r * t2 + t1) * k2) % n).astype(F32)
    c, s = jnp.cos(ang), jnp.sin(ang)
    if full_input:
        return jnp.concatenate([c, -s], axis=1).astype(BF16)
    top = jnp.concatenate([c, s], axis=2)
    bot = jnp.concatenate([-s, c], axis=2)
    mats = jnp.concatenate([top, bot], axis=1)
    mats_t = jnp.swapaxes(mats, 1, 2) * (1.0 / n)
    return mats.astype(BF16), mats_t.astype(BF16)


def _dft_radix():
    r = FFT_RADIX
    a = jnp.arange(r, dtype=I32)
    ang = (2.0 * math.pi / r) * ((a[:, None] * a[None, :]) % r).astype(F32)
    c, s = jnp.cos(ang), jnp.sin(ang)
    f = jnp.concatenate([jnp.concatenate([c, s], 1), jnp.concatenate([-s, c], 1)], 0)
    return f.astype(BF16), f.T.astype(BF16)


def _dense_dft(n, n_in, full_input):
    k = jnp.arange(n, dtype=I32)[:, None]
    t = jnp.arange(n_in, dtype=I32)[None, :]
    ang = (2.0 * math.pi / n) * ((k * t) % n).astype(F32)
    c, s = jnp.cos(ang), jnp.sin(ang)
    if full_input:
        return jnp.concatenate([c, -s], 0).astype(BF16)
    fwd = jnp.concatenate([jnp.concatenate([c, s], 1), jnp.concatenate([-s, c], 1)], 0)
    return fwd.astype(BF16), (fwd.T * (1.0 / n)).astype(BF16)


def hyena_long(vx, filt, l1, skip):
    _, b, seq, c = vx.shape
    assert b == 2
    r = FFT_RADIX
    inv_l1 = 1.0 / l1[:2]
    if 2 * seq == r * r:
        mats, mats_t = _dft_tables(False)
        mats_f = _dft_tables(True)
        fmat, fmat_t = _dft_radix()
        v3, x1, x2 = (vx[i].reshape(r, r, c) for i in range(3))
        spec = [fft_stage_mid(fft_stage_in(filt[o].reshape(r, r, c), mats_f), fmat, fmat_t,
                              scale=inv_l1[o:o + 1]) for o in range(2)]
        z3 = fft_stage_out(fft_stage_mid(fft_stage_in(v3, mats), fmat, fmat_t, h_pk=spec[0]),
                           mats_t, v3, x1, skip[0:1], F32)
        y3 = fft_stage_out(fft_stage_mid(fft_stage_in(z3, mats), fmat, fmat_t, h_pk=spec[1]),
                           mats_t, z3, x2, skip[1:2], BF16)
        return y3.reshape(b, seq, c)
    n = 2 * seq
    fwd, inv = _dense_dft(n, seq, False)
    fwd_f = _dense_dft(n, n, True)
    v, x1, x2 = (vx[i].reshape(b * seq, c) for i in range(3))

    def conv(u, o):
        hs = matmul([fwd_f], filt[o][None], 0, n_out=c) * inv_l1[o:o + 1]
        xs = matmul([fwd], u[None], 0, n_out=c)
        hr, hi, xr, xi = hs[:n], hs[n:], xs[:n], xs[n:]
        ys = jnp.concatenate([xr * hr - xi * hi, xr * hi + xi * hr], 0)
        return matmul([inv], ys[None], 0, n_out=c)

    z = x1 * (conv(v, 0) + v * skip[0:1])
    y = x2 * (conv(z, 1) + z * skip[1:2])
    return y.astype(BF16).reshape(b, seq, c)


def _dwconv_kernel(x_ref, p_ref, n_ref, w_ref, b_ref, g_ref, be_ref, o_ref, win_ref, y_ref, sh_ref):
    i = pl.program_id(1)
    nt = pl.num_programs(1)
    ts = x_ref.shape[1]
    d = x_ref.shape[2]
    win_ref[0:HALO, :] = jnp.where(i == 0, 0.0, p_ref[0])
    win_ref[HALO:HALO + ts, :] = x_ref[0]
    win_ref[HALO + ts:, :] = jnp.where(i == nt - 1, 0.0, n_ref[0])
    pad = (CONF_K - 1) // 2
    first = HALO - pad
    rows = DW_ROWS
    def lane_block(cb, carry):
        cs = pl.ds(pl.multiple_of(cb * LANES, LANES), LANES)
        wb = [jnp.broadcast_to(w_ref[j:j + 1, cs], (SUBLANES, LANES)) for j in range(CONF_K)]
        bb = jnp.broadcast_to(b_ref[:, cs], (SUBLANES, LANES))
        for r0 in range(0, ts, rows):
            acc = jnp.zeros((rows // SUBLANES, SUBLANES, LANES), F32)
            for r in range(SUBLANES):
                offs = [o for o in range(first, first + CONF_K) if o % SUBLANES == r]
                span = rows + (max(offs) // SUBLANES) * SUBLANES
                sh_ref[r, 0:span, :] = win_ref[pl.ds(r0 + r, span), cs]
                for o in offs:
                    a8 = (o // SUBLANES) * SUBLANES
                    tap = sh_ref[r, a8:a8 + rows, :].reshape(rows // SUBLANES, SUBLANES, LANES)
                    acc = acc + tap * wb[o - first]
            y_ref[r0:r0 + rows, cs] = (acc + bb).reshape(rows, LANES)
        return carry

    lax.fori_loop(0, d // LANES, lane_block, 0)
    y = y_ref[...]
    mu = jnp.mean(y, axis=-1, keepdims=True)
    yc = y - mu
    var = jnp.mean(yc * yc, axis=-1, keepdims=True)
    a = yc * lax.rsqrt(var + NORM_EPS) * g_ref[...] + be_ref[...]
    o_ref[0] = (a * jax.nn.sigmoid(a)).astype(o_ref.dtype)


def conformer_dwconv(x, w, b, g, be):
    bsz, s, d = x.shape
    ts = LN_ROWS
    hb = ts // HALO
    nhb = s // HALO
    vec = lambda r: BS((r, d), lambda bi, i: (0, 0))
    return pl.pallas_call(
        _dwconv_kernel,
        out_shape=SDS((bsz, s, d), BF16),
        grid=(bsz, s // ts),
        in_specs=[BS((1, ts, d), lambda bi, i: (bi, i, 0)),
                  BS((1, HALO, d), lambda bi, i: (bi, jnp.maximum(i * hb - 1, 0), 0)),
                  BS((1, HALO, d), lambda bi, i: (bi, jnp.minimum((i + 1) * hb, nhb - 1), 0)),
                  vec(CONF_K), vec(1), vec(1), vec(1)],
        out_specs=BS((1, ts, d), lambda bi, i: (bi, i, 0)),
        scratch_shapes=[pltpu.VMEM((ts + 2 * HALO, d), F32), pltpu.VMEM((ts, d), F32),
                        pltpu.VMEM((SUBLANES, DW_ROWS + 2 * HALO, LANES), F32)],
        compiler_params=_params(("parallel", "parallel")),
        name="conformer_dwconv",
    )(x, x, x, w, b.reshape(1, d), g.reshape(1, d), be.reshape(1, d))


def _layer_norm(v, g, b):
    mu = jnp.mean(v, axis=-1, keepdims=True)
    vc = v - mu
    var = jnp.mean(vc * vc, axis=-1, keepdims=True)
    return vc * lax.rsqrt(var + NORM_EPS) * g + b


def _route(tok, rw_ref, rb_ref):
    rows = tok.shape[0]
    logits = _dot3(tok, rw_ref[0], rw_ref[1])
    aff = jax.nn.sigmoid(logits)
    sel = aff + rb_ref[...]
    sc = [sel[:, e:e + 1] for e in range(N_EXPERTS)]
    ac = [aff[:, e:e + 1] for e in range(N_EXPERTS)]
    scores = []
    for gi in range(N_GROUPS):
        m = sc[gi * 4:(gi + 1) * 4]
        best = m[0] + m[1]
        for a in range(4):
            for bb in range(a + 1, 4):
                if (a, bb) != (0, 1):
                    best = jnp.maximum(best, m[a] + m[bb])
        scores.append(best)
    top = scores[0]
    grp = jnp.zeros((rows, 1), I32)
    for gi in range(1, N_GROUPS):
        upd = scores[gi] > top
        grp = jnp.where(upd, gi, grp)
        top = jnp.where(upd, scores[gi], top)

    def pick(cols, k):
        out = cols[k]
        for gi in range(1, N_GROUPS):
            out = jnp.where(grp == gi, cols[gi * 4 + k], out)
        return out

    ms = [pick(sc, k) for k in range(4)]
    ma = [pick(ac, k) for k in range(4)]

    def argmax4(vals):
        bv, bi, ba = vals[0], jnp.zeros((rows, 1), I32), ma[0]
        for k in range(1, 4):
            upd = vals[k] > bv
            bi = jnp.where(upd, k, bi)
            ba = jnp.where(upd, ma[k], ba)
            bv = jnp.where(upd, vals[k], bv)
        return bi, ba

    i1, a1 = argmax4(ms)
    i2, a2 = argmax4([jnp.where(i1 == k, -jnp.inf, ms[k]) for k in range(4)])
    tot = a1 + a2
    return grp * 4 + i1, grp * 4 + i2, a1 / tot, a2 / tot


def _ln_router_kernel(x_ref, y_ref, g_ref, lg_ref, lb_ref, sc_ref, sh_ref, rw_ref, rb_ref,
                      xo_ref, tok_ref, rt_ref):
    v = DEEPNORM_ALPHA * x_ref[0] + g_ref[0] * y_ref[0].astype(F32)
    xn = _layer_norm(v, lg_ref[...], lb_ref[...])
    xo_ref[0] = xn
    tok = xn * (1.0 + sc_ref[0]) + sh_ref[0]
    half = tok.shape[1] // 2
    tok_ref[0] = _pack(tok[:, :half], tok[:, half:])
    e1, e2, g1, g2 = _route(tok, rw_ref, rb_ref)
    lane = lax.broadcasted_iota(I32, rt_ref.shape[1:], 1)
    rt_ref[0] = jnp.where(lane == 0, e1.astype(F32),
                          jnp.where(lane == 1, e2.astype(F32),
                                    jnp.where(lane == 2, g1, jnp.where(lane == 3, g2, 0.0))))


def ln_router(x, y, gate, ln_g, ln_b, sc, sh, rw_pad, rb_pad):
    b, s, d = x.shape
    ts = min(LN_ROWS, s)
    tile = BS((1, ts, d), lambda bi, i: (bi, i, 0))
    vec = BS((1, 1, d), lambda bi, i: (bi, 0, 0))
    par = BS((1, d), lambda bi, i: (0, 0))
    return pl.pallas_call(
        _ln_router_kernel,
        out_shape=[SDS((b, s, d), F32), SDS((b, s, d // 2), U32), SDS((b, s, LANES), F32)],
        grid=(b, s // ts),
        in_specs=[tile, tile, vec, par, par, vec, vec,
                  BS((2, d, LANES), lambda bi, i: (0, 0, 0)), BS((1, LANES), lambda bi, i: (0, 0))],
        out_specs=[tile, BS((1, ts, d // 2), lambda bi, i: (bi, i, 0)), BS((1, ts, LANES), lambda bi, i: (bi, i, 0))],
        compiler_params=_params(("parallel", "parallel")),
        name="ln_router",
    )(x, y.reshape(b, s, d), gate, ln_g.reshape(1, d), ln_b.reshape(1, d), sc, sh, rw_pad, rb_pad)


def _row_copy_wait(src_rows_ref, dst_any, sem, n_rows):
    pltpu.make_async_copy(src_rows_ref, dst_any.at[pl.ds(0, n_rows), :], sem).wait()


def _dispatch_kernel(pos_ref, lt_ref, na_ref, lat_ref, ctx_ref, xs_hbm, zero_ref, sem, zsem, *, n_lat, n_tok):
    i = pl.program_id(0)
    ts = lat_ref.shape[0]
    n_tiles = xs_hbm.shape[0] // GMM_ROWS

    def zero_tile(t):
        start = pl.multiple_of(t * GMM_ROWS, GMM_ROWS)
        return pltpu.make_async_copy(zero_ref, xs_hbm.at[pl.ds(start, GMM_ROWS), :], zsem)

    @pl.when(i == 0)
    def _():
        zero_ref[...] = jnp.zeros_like(zero_ref)
        for e in range(N_EXPERTS):
            @pl.when(lt_ref[e] >= 0)
            def _():
                zero_tile(lt_ref[e]).start()

        def start_tail(t, c):
            zero_tile(t).start()
            return c

        lax.fori_loop(na_ref[0], n_tiles, start_tail, 0)
        for e in range(N_EXPERTS):
            @pl.when(lt_ref[e] >= 0)
            def _():
                zero_tile(0).wait()

        def wait_tail(t, c):
            zero_tile(0).wait()
            return c

        lax.fori_loop(na_ref[0], n_tiles, wait_tail, 0)

    def scatter(src_ref):
        base = i * ts

        def body(r, c):
            for k in range(2):
                dst = pos_ref[k * n_tok + base + r]
                pltpu.make_async_copy(src_ref.at[pl.ds(r, 1), :], xs_hbm.at[pl.ds(dst, 1), :], sem).start()
            return c

        lax.fori_loop(0, ts, body, 0, unroll=ROW_COPY_UNROLL)
        for _ in range(2):
            _row_copy_wait(src_ref, xs_hbm, sem, ts)

    if ctx_ref is None:
        scatter(lat_ref)
    else:
        @pl.when(i < n_lat)
        def _():
            scatter(lat_ref)

        @pl.when(i >= n_lat)
        def _():
            scatter(ctx_ref)


def dispatch(pos, last_tile, n_active, tok_lat, tok_ctx, n_rows):
    t_lat, d = tok_lat.shape
    ts = LN_ROWS
    n_lat = t_lat // ts
    n_ctx = 0 if tok_ctx is None else tok_ctx.shape[0] // ts
    n_tok = t_lat + (0 if tok_ctx is None else tok_ctx.shape[0])
    in_specs = [BS((ts, d), lambda i, p, l, na: (jnp.minimum(i, n_lat - 1), 0))]
    args = [tok_lat]
    if tok_ctx is not None:
        in_specs.append(BS((ts, d), lambda i, p, l, na: (jnp.maximum(i - n_lat, 0), 0)))
        args.append(tok_ctx)

    def kern(pos_ref, lt_ref, na_ref, *refs):
        if tok_ctx is None:
            lat_ref, xs_hbm, zero_ref, sem, zsem = refs
            ctx_ref = None
        else:
            lat_ref, ctx_ref, xs_hbm, zero_ref, sem, zsem = refs
        _dispatch_kernel(pos_ref, lt_ref, na_ref, lat_ref, ctx_ref, xs_hbm, zero_ref, sem, zsem,
                         n_lat=n_lat, n_tok=n_tok)

    return pl.pallas_call(
        kern,
        out_shape=SDS((n_rows, d), tok_lat.dtype),
        grid_spec=pltpu.PrefetchScalarGridSpec(
            num_scalar_prefetch=3, grid=(n_lat + n_ctx,),
            in_specs=in_specs,
            out_specs=BS(memory_space=pl.ANY),
            scratch_shapes=[pltpu.VMEM((GMM_ROWS, d), tok_lat.dtype), pltpu.SemaphoreType.DMA(()),
                            pltpu.SemaphoreType.DMA(())]),
        compiler_params=_params(("arbitrary",)),
        name="dispatch",
    )(pos, last_tile, n_active, *args)


def _gmm_kernel(te_ref, na_ref, x_ref, w1_ref, w3_ref, w2_ref, o_ref, acc_ref):
    i = pl.program_id(0)
    j = pl.program_id(1)
    dh = x_ref.shape[1]

    @pl.when(i < na_ref[0])
    def _():
        xa, xb = _unpack(x_ref[...])
        xa = xa.astype(BF16)
        xb = xb.astype(BF16)

        def up(w_ref):
            return (jnp.dot(xa, w_ref[0, 0, :dh, :].astype(BF16), preferred_element_type=F32)
                    + jnp.dot(xb, w_ref[0, 0, dh:, :].astype(BF16), preferred_element_type=F32))

        a = up(w1_ref)
        b = up(w3_ref)
        h = (a * jax.nn.sigmoid(a) * b).astype(BF16)
        part = jnp.dot(h, w2_ref[0, 0].astype(BF16), preferred_element_type=F32)

        @pl.when(j == 0)
        def _():
            acc_ref[...] = part

        @pl.when(j != 0)
        def _():
            tot = acc_ref[...] + part
            o_ref[...] = _pack(tot[:, :dh], tot[:, dh:])

    @pl.when(jnp.logical_and(i >= na_ref[0], j == 0))
    def _():
        o_ref[...] = jnp.zeros_like(o_ref)


def grouped_mlp(tile_expert, n_active, xs, w1, w3, w2, layer):
    n_rows, dh = xs.shape
    d = 2 * dh
    n_tiles = n_rows // GMM_ROWS

    def tile(i, te, na):
        return jnp.minimum(i, na[0] - 1)

    def half(i, j, na):
        last = na[0] - 1
        return jnp.where(i <= last, jnp.where(i % 2 == 0, j, 1 - j), jnp.where(last % 2 == 0, 1, 0))

    def expert(i, te, na):
        return te[tile(i, te, na)]

    return pl.pallas_call(
        _gmm_kernel,
        out_shape=SDS((n_rows, dh), U32),
        grid_spec=pltpu.PrefetchScalarGridSpec(
            num_scalar_prefetch=2, grid=(n_tiles, 2),
            in_specs=[BS((GMM_ROWS, dh), lambda i, j, te, na: (tile(i, te, na), 0)),
                      BS((1, 1, d, GMM_HALF), lambda i, j, te, na: (layer, expert(i, te, na), 0, half(i, j, na))),
                      BS((1, 1, d, GMM_HALF), lambda i, j, te, na: (layer, expert(i, te, na), 0, half(i, j, na))),
                      BS((1, 1, GMM_HALF, d), lambda i, j, te, na: (layer, expert(i, te, na), half(i, j, na), 0))],
            out_specs=BS((GMM_ROWS, dh), lambda i, j, te, na: (i, 0)),
            scratch_shapes=[pltpu.VMEM((GMM_ROWS, d), F32)]),
        compiler_params=_params(("arbitrary", "arbitrary")),
        name="grouped_mlp",
    )(tile_expert, n_active, xs, w1, w3, w2)


def _combine_ln_kernel(pos_ref, x_ref, rt_ref, g_ref, lg_ref, lb_ref, sc_ref, sh_ref, ys_hbm,
                       xo_ref, h_ref, buf_ref, sem, *, tok_off, n_tok, tiles_per_batch, emit_h):
    ts = x_ref.shape[1]
    step = pl.program_id(0) * tiles_per_batch + pl.program_id(1)
    n_steps = pl.num_programs(0) * tiles_per_batch
    slot = step % 2

    def gather(tile, dst_slot):
        base = tok_off + tile * ts

        def body(r, c):
            for k in range(2):
                src = pos_ref[k * n_tok + base + r]
                pltpu.make_async_copy(ys_hbm.at[pl.ds(src, 1), :], buf_ref.at[dst_slot, k, pl.ds(r, 1), :],
                                      sem.at[dst_slot]).start()
            return c

        lax.fori_loop(0, ts, body, 0, unroll=ROW_COPY_UNROLL)

    @pl.when(step == 0)
    def _():
        gather(step, slot)

    @pl.when(step + 1 < n_steps)
    def _():
        gather(step + 1, 1 - slot)

    for k in range(2):
        pltpu.make_async_copy(ys_hbm.at[pl.ds(0, ts), :], buf_ref.at[slot, k], sem.at[slot]).wait()
    rt = rt_ref[0]
    halves = []
    for part in range(2):
        y0 = _unpack(buf_ref[slot, 0])[part]
        y1 = _unpack(buf_ref[slot, 1])[part]
        halves.append(rt[:, 2:3] * y0 + rt[:, 3:4] * y1)
    moe = jnp.concatenate(halves, axis=1)
    xn = _layer_norm(DEEPNORM_ALPHA * x_ref[0] + g_ref[0] * moe, lg_ref[...], lb_ref[...])
    xo_ref[0] = xn
    if emit_h:
        h_ref[0] = (xn * (1.0 + sc_ref[0]) + sh_ref[0]).astype(h_ref.dtype)
    else:
        h_ref[0] = jnp.zeros(h_ref.shape[1:], h_ref.dtype)


def combine_ln(pos, x, route, gate, ln_g, ln_b, sc, sh, ys, *, tok_off, n_tok, emit_h=True):
    b, s, d = x.shape
    ts = min(LN_ROWS, s)
    tpb = s // ts
    tile = BS((1, ts, d), lambda bi, i, p: (bi, i, 0))
    vec = BS((1, 1, d), lambda bi, i, p: (bi, 0, 0))
    par = BS((1, d), lambda bi, i, p: (0, 0))
    h_rows = ts if emit_h else SUBLANES
    return pl.pallas_call(
        functools.partial(_combine_ln_kernel, tok_off=tok_off, n_tok=n_tok, tiles_per_batch=tpb, emit_h=emit_h),
        out_shape=[SDS((b, s, d), F32), SDS((b, tpb * h_rows, d), BF16)],
        grid_spec=pltpu.PrefetchScalarGridSpec(
            num_scalar_prefetch=1, grid=(b, tpb),
            in_specs=[tile, BS((1, ts, LANES), lambda bi, i, p: (bi, i, 0)), vec, par, par, vec, vec,
                      BS(memory_space=pl.ANY)],
            out_specs=[tile, BS((1, h_rows, d), lambda bi, i, p: (bi, i, 0))],
            scratch_shapes=[pltpu.VMEM((2, 2, ts, d // 2), U32), pltpu.SemaphoreType.DMA((2,))]),
        compiler_params=_params(("arbitrary", "arbitrary")),
        name="combine_ln",
    )(pos, x, route, gate, ln_g.reshape(1, d), ln_b.reshape(1, d), sc, sh, ys)


def _rope_tables(n_tokens):
    rows = n_tokens // GRID_W
    row = jnp.repeat(jnp.arange(rows, dtype=F32), GRID_W)
    col = jnp.tile(jnp.arange(GRID_W, dtype=F32), rows)
    half = HEAD_DIM // 2
    inv = ROPE_THETA ** (-jnp.arange(0, half, 2, dtype=F32) / half)
    ang = jnp.concatenate([row[:, None] * inv, col[:, None] * inv], -1)
    cos, sin = jnp.cos(ang), jnp.sin(ang)
    return jnp.concatenate([cos, cos], -1), jnp.concatenate([-sin, sin], -1)


def _head_perm():
    base = jnp.concatenate([jnp.arange(0, HEAD_DIM, 2), jnp.arange(1, HEAD_DIM, 2)])
    n_heads = N_Q_HEADS + N_KV_HEADS
    return base, jnp.concatenate([h * HEAD_DIM + base for h in range(n_heads)])


def _even_mixer(h, hc, ctx_out, cos_t, sin_t, w_in, q_g, k_g, conv_w, conv_b,
                f_w1, f_b1, f_w2, f_b2, f_w3, f_freq, skip, w_out_all, li):
    b, s, d = h.shape
    lc = hc.shape[1]
    base, perm = _head_perm()
    n_perm = perm.shape[0]
    w_in_p = jnp.concatenate([w_in[:, :n_perm][:, perm], w_in[:, n_perm:]], axis=1)[None]
    qg = q_g[base].reshape(1, HEAD_DIM)
    kg = k_g[base].reshape(1, HEAD_DIM)
    n_in = w_in.shape[1]

    proj = matmul([h.reshape(b * s, d)], w_in_p, 0, n_out=n_in, tn=768).reshape(b, s, n_in)
    proj_c = matmul([hc.reshape(b * lc, d)], w_in_p, 0, n_out=n_in, tn=768).reshape(b, lc, n_in)
    q, k, v = qkv_prepare(proj, qg, kg, cos_t, sin_t, rope=True, with_q=True)
    qc, kc, vc = qkv_prepare(proj_c, qg, kg, None, None, rope=False, with_q=ctx_out)
    att = attention(q, jnp.concatenate([k, kc], axis=1), jnp.concatenate([v, vc], axis=1))

    def hyena(p, seq):
        vx = hyena_short_conv(p, conv_w, conv_b)
        filt, l1 = hyena_filter(seq, f_w1, f_b1, f_w2, f_b2, f_w3, f_freq)
        return hyena_long(vx, filt, l1, skip)

    hy = hyena(proj, s)
    y = matmul([att.reshape(b * s, ATTN_WIDTH), hy.reshape(b * s, HYENA_WIDTH)], w_out_all, li,
               n_out=d).reshape(b, s, d)
    if not ctx_out:
        return y, None
    att_c = attention(qc, kc, vc)
    hy_c = hyena(proj_c, lc)
    yc = matmul([att_c.reshape(b * lc, ATTN_WIDTH), hy_c.reshape(b * lc, HYENA_WIDTH)], w_out_all, li,
                n_out=d).reshape(b, lc, d)
    return y, yc


def _conformer(h, in_w, in_b, dw_w, dw_b, n_g, n_b, out_w, out_b, li):
    b, s, d = h.shape
    a = matmul([h.reshape(b * s, d)], in_w, li, n_out=d, bias=in_b, glu=True).reshape(b, s, d)
    a = conformer_dwconv(a, dw_w[li], dw_b[li], n_g[li], n_b[li])
    return matmul([a.reshape(b * s, d)], out_w, li, n_out=d, bias=out_b).reshape(b, s, d)


def _dispatch_plan(experts, n_tiles):
    n_tok = experts.shape[0]
    flat = experts.T.reshape(-1)
    onehot = (flat[:, None] == jnp.arange(N_EXPERTS, dtype=I32)[None, :]).astype(I32)
    csum = jnp.cumsum(onehot, axis=0)
    rank = jnp.take_along_axis(csum - onehot, flat[:, None], axis=1)[:, 0]
    counts = csum[-1]
    padded = ((counts + GMM_ROWS - 1) // GMM_ROWS) * GMM_ROWS
    ends = jnp.cumsum(padded)
    starts = ends - padded
    pos = (starts[flat] + rank).astype(I32)
    n_active = (ends[-1] // GMM_ROWS).astype(I32).reshape(1)
    tile_start = jnp.arange(n_tiles, dtype=I32) * GMM_ROWS
    tile_expert = jnp.minimum(jnp.sum((ends[None, :] <= tile_start[:, None]).astype(I32), axis=1),
                              N_EXPERTS - 1).astype(I32)
    last_tile = jnp.where(padded > 0, ends // GMM_ROWS - 1, -1).astype(I32)
    del n_tok
    return pos, n_active, tile_expert, last_tile


def kernel(x, c, ctx, c_ctx, mod_w, mod_b, ln_g, ln_b, mix_in_w, q_norm_g, k_norm_g, hy_conv_w, hy_conv_b,
           hy_w1, hy_b1, hy_w2, hy_b2, hy_w3, hy_freq, hy_skip, mix_out_w, cf_in_w, cf_in_b, cf_dw_w, cf_dw_b,
           cf_ln_g, cf_ln_b, cf_out_w, cf_out_b, router_w, router_b, exp_w1, exp_w3, exp_w2):
    b, s, d = x.shape
    lc = ctx.shape[1]
    cos_t, sin_t = _rope_tables(s)
    cvec = jnp.concatenate([c, c_ctx[None], jnp.zeros((SUBLANES - b - 1, d), F32)], 0)
    mods = mod_vectors(cvec, mod_w, mod_b)
    rw_pad = jnp.stack(_split(jnp.concatenate([router_w, jnp.zeros((d, LANES - N_EXPERTS), F32)], 1)))
    rb_pad = jnp.concatenate([router_b, jnp.zeros((LANES - N_EXPERTS,), F32)]).reshape(1, LANES)

    def mod_of(l, j, is_ctx):
        m = mods[l, :, j * d:(j + 1) * d]
        rows = jnp.broadcast_to(m[b:b + 1], (b, d)) if is_ctx else m[:b]
        return rows.reshape(b, 1, d)

    h = modulate(x, mod_of(0, 1, False), mod_of(0, 0, False))
    hc = modulate(ctx, mod_of(0, 1, True), mod_of(0, 0, True))
    for l in range(DEPTH):
        ctx_needed = any(j > l and j % 2 == 0 for j in range(DEPTH))
        li = l // 2
        if l % 2 == 0:
            y, yc = _even_mixer(h, hc, ctx_needed, cos_t, sin_t, mix_in_w[li], q_norm_g[li], k_norm_g[li],
                                hy_conv_w[li], hy_conv_b[li], hy_w1[li], hy_b1[li], hy_w2[li], hy_b2[li],
                                hy_w3[li], hy_freq[li], hy_skip[li], mix_out_w, li)
        else:
            cf = (cf_in_w, cf_in_b, cf_dw_w, cf_dw_b, cf_ln_g, cf_ln_b, cf_out_w, cf_out_b)
            y = _conformer(h, *cf, li)
            yc = _conformer(hc, *cf, li) if ctx_needed else None

        x, tok, route = ln_router(x, y, mod_of(l, 2, False), ln_g[l, 0], ln_b[l, 0],
                                  mod_of(l, 4, False), mod_of(l, 3, False), rw_pad, rb_pad)
        routes = [route.reshape(b * s, LANES)]
        tok_c = None
        if ctx_needed:
            ctx, tok_c, route_c = ln_router(ctx, yc, mod_of(l, 2, True), ln_g[l, 0], ln_b[l, 0],
                                            mod_of(l, 4, True), mod_of(l, 3, True), rw_pad, rb_pad)
            routes.append(route_c.reshape(b * lc, LANES))
            tok_c = tok_c.reshape(b * lc, d // 2)
        n_tok = b * s + (b * lc if ctx_needed else 0)
        n_tiles = (2 * n_tok) // GMM_ROWS + N_EXPERTS
        experts = jnp.concatenate([r[:, :2] for r in routes], 0).astype(I32)
        pos, n_active, tile_expert, last_tile = _dispatch_plan(experts, n_tiles)
        xs = dispatch(pos, last_tile, n_active, tok.reshape(b * s, d // 2), tok_c, n_tiles * GMM_ROWS)
        ys = grouped_mlp(tile_expert, n_active, xs, exp_w1, exp_w3, exp_w2, l)

        last = l == DEPTH - 1
        nl = min(l + 1, DEPTH - 1)
        x, h = combine_ln(pos, x, route, mod_of(l, 5, False), ln_g[l, 1], ln_b[l, 1],
                          mod_of(nl, 1, False), mod_of(nl, 0, False), ys, tok_off=0, n_tok=n_tok,
                          emit_h=not last)
        if ctx_needed:
            ctx, hc = combine_ln(pos, ctx, route_c, mod_of(l, 5, True), ln_g[l, 1], ln_b[l, 1],
                                 mod_of(nl, 1, True), mod_of(nl, 0, True), ys, tok_off=b * s, n_tok=n_tok)
    return x
```

```python
import functools
import math

import jax
import jax.numpy as jnp
from jax import lax
from jax.experimental import pallas as pl
from jax.experimental.pallas import tpu as pltpu

F32 = jnp.float32
BF16 = jnp.bfloat16
U32 = jnp.uint32
I32 = jnp.int32
HIGHEST = lax.Precision.HIGHEST
SDS = jax.ShapeDtypeStruct
BS = pl.BlockSpec

D_MODEL = 2048
DEPTH = 4
GRID_W = 64
HEAD_DIM = 128
N_Q_HEADS = 8
N_KV_HEADS = 2
Q_PER_KV = N_Q_HEADS // N_KV_HEADS
ATTN_WIDTH = N_Q_HEADS * HEAD_DIM
KV_WIDTH = N_KV_HEADS * HEAD_DIM
ATTN_SCALE = HEAD_DIM ** -0.5
ROPE_THETA = 10000.0
HYENA_WIDTH = D_MODEL - ATTN_WIDTH
HYENA_EMB_DIM = 33
HYENA_FILTER_HIDDEN = 64
HYENA_FAST_DECAY_PCT = 0.3
HYENA_SLOW_DECAY_PCT = 1.5
HYENA_TARGET = 1e-2
U_OFFSET = ATTN_WIDTH + 2 * KV_WIDTH
CONF_K = 31
N_EXPERTS = 16
N_GROUPS = 4
EXPERTS_PER_GROUP = 4
D_EXPERT = 1024
DEEPNORM_ALPHA = (2 * DEPTH) ** 0.25
N_MOD = 6
NORM_EPS = 1e-6
LOG2_E = math.log2(math.e)

LANES = 128
SUBLANES = 8
VMEM_LIMIT_BYTES = 56 * 1024 * 1024

LN_ROWS = 256
MM_ROWS = 1024
GMM_ROWS = 512
GMM_HALF = D_EXPERT // 2
FFT_RADIX = 128
FFT_GROUP = 8
ATT_Q_ROWS = 512
ATT_K_ROWS = 768
VT_ROWS = HEAD_DIM + 16
HALO = 16
DW_ROWS = 64
ROW_COPY_UNROLL = 8


def _params(semantics):
    return pltpu.CompilerParams(dimension_semantics=semantics, vmem_limit_bytes=VMEM_LIMIT_BYTES)


def _split(x):
    hi = x.astype(BF16)
    return hi, (x - hi.astype(F32)).astype(BF16)


def _dot3(x, w_hi, w_lo):
    x_hi, x_lo = _split(x)
    dot = functools.partial(jnp.dot, preferred_element_type=F32)
    return dot(x_hi, w_hi) + (dot(x_hi, w_lo) + dot(x_lo, w_hi))


def _mod_kernel(c_ref, w_ref, b_ref, o_ref):
    c = c_ref[...]
    s = c * jax.nn.sigmoid(c)
    o_ref[0] = jnp.dot(s, w_ref[0], precision=HIGHEST, preferred_element_type=F32) + b_ref[0]


def mod_vectors(cvec, mod_w, mod_b):
    depth, d, n = mod_w.shape
    tn = 1024
    return pl.pallas_call(
        _mod_kernel,
        out_shape=SDS((depth, SUBLANES, n), F32),
        grid=(depth, n // tn),
        in_specs=[BS((SUBLANES, d), lambda l, j: (0, 0)),
                  BS((1, d, tn), lambda l, j: (l, 0, j)),
                  BS((1, 1, tn), lambda l, j: (l, 0, j))],
        out_specs=BS((1, SUBLANES, tn), lambda l, j: (l, 0, j)),
        compiler_params=_params(("parallel", "parallel")),
        name="mod_vectors",
    )(cvec, mod_w, mod_b.reshape(depth, 1, n))


def _modulate_kernel(x_ref, sc_ref, sh_ref, o_ref):
    o_ref[0] = (x_ref[0] * (1.0 + sc_ref[0]) + sh_ref[0]).astype(o_ref.dtype)


def modulate(x, sc, sh):
    b, s, d = x.shape
    ts = min(LN_ROWS, s)
    vec = BS((1, 1, d), lambda bi, i: (bi, 0, 0))
    return pl.pallas_call(
        _modulate_kernel,
        out_shape=SDS((b, s, d), BF16),
        grid=(b, s // ts),
        in_specs=[BS((1, ts, d), lambda bi, i: (bi, i, 0)), vec, vec],
        out_specs=BS((1, ts, d), lambda bi, i: (bi, i, 0)),
        compiler_params=_params(("parallel", "parallel")),
        name="modulate",
    )(x, sc, sh)


def _mm_kernel(*refs, k_sizes, has_bias, glu):
    n_a = len(k_sizes)
    a_refs = refs[:n_a]
    pos = n_a
    w_refs = refs[pos:pos + (2 if glu else 1)]
    pos += len(w_refs)
    b_refs = ()
    if has_bias:
        b_refs = refs[pos:pos + len(w_refs)]
        pos += len(w_refs)
    o_ref = refs[pos]
    wbf_refs = refs[pos + 1:]

    @pl.when(pl.program_id(1) == 0)
    def _():
        for w_ref, wbf in zip(w_refs, wbf_refs):
            wbf[...] = w_ref[0].astype(BF16)

    def linear(which):
        acc = None
        off = 0
        for a_ref, k in zip(a_refs, k_sizes):
            part = jnp.dot(a_ref[...].astype(BF16), wbf_refs[which][off:off + k, :],
                           preferred_element_type=F32)
            acc = part if acc is None else acc + part
            off += k
        if has_bias:
            acc = acc + b_refs[which][0]
        return acc

    out = linear(0)
    if glu:
        out = out * jax.nn.sigmoid(linear(1))
    o_ref[...] = out.astype(o_ref.dtype)


def matmul(a_list, w, layer, *, n_out, col_off=0, bias=None, glu=False, out_dtype=F32, tn=512):
    m = a_list[0].shape[0]
    k_sizes = tuple(a.shape[1] for a in a_list)
    k_total = sum(k_sizes)
    assert w.shape[1] == k_total and n_out % tn == 0 and col_off % tn == 0
    tm = min(MM_ROWS, m)
    assert m % tm == 0
    nb = n_out // tn
    cb = col_off // tn
    in_specs = [BS((tm, k), lambda j, i: (i, 0)) for k in k_sizes]
    args = list(a_list)
    n_w = 2 if glu else 1
    for h in range(n_w):
        in_specs.append(BS((1, k_total, tn), lambda j, i, h=h: (layer, 0, cb + h * nb + j)))
        args.append(w)
    if bias is not None:
        bias3 = bias.reshape(bias.shape[0], 1, bias.shape[1])
        for h in range(n_w):
            in_specs.append(BS((1, 1, tn), lambda j, i, h=h: (layer, 0, cb + h * nb + j)))
            args.append(bias3)
    return pl.pallas_call(
        functools.partial(_mm_kernel, k_sizes=k_sizes, has_bias=bias is not None, glu=glu),
        out_shape=SDS((m, n_out), out_dtype),
        grid=(nb, m // tm),
        in_specs=in_specs,
        out_specs=BS((tm, tn), lambda j, i: (i, j)),
        scratch_shapes=[pltpu.VMEM((k_total, tn), BF16) for _ in range(n_w)],
        compiler_params=_params(("parallel", "arbitrary")),
        name="matmul_glu" if glu else "matmul",
    )(*args)


def _qkv_kernel(*refs, rope, n_q_heads, n_k_heads):
    it = iter(refs)
    q_ref = next(it) if n_q_heads else None
    kv_ref = next(it)
    qg_ref = next(it)
    kg_ref = next(it)
    cos_ref = sin_ref = None
    if rope:
        cos_ref = next(it)
        sin_ref = next(it)
    qo_ref = next(it) if n_q_heads else None
    ko_ref = next(it)
    vo_ref = next(it)

    def prep(xh, g, scale):
        ms = jnp.mean(xh * xh, axis=-1, keepdims=True)
        xn = xh * lax.rsqrt(ms + NORM_EPS) * g
        if rope:
            xn = xn * cos_ref[...] + pltpu.roll(xn, HEAD_DIM // 2, 1) * sin_ref[...]
        if scale != 1.0:
            xn = xn * scale
        return xn.astype(BF16)

    for h in range(n_q_heads):
        sl = slice(h * HEAD_DIM, (h + 1) * HEAD_DIM)
        qo_ref[0, :, sl] = prep(q_ref[0, :, sl], qg_ref[...], ATTN_SCALE * LOG2_E)
    for h in range(n_k_heads):
        sl = slice(h * HEAD_DIM, (h + 1) * HEAD_DIM)
        ko_ref[0, :, sl] = prep(kv_ref[0, :, sl], kg_ref[...], 1.0)
    vo_ref[0] = kv_ref[0, :, KV_WIDTH:].astype(BF16)


def qkv_prepare(proj, q_gain, k_gain, cos_t, sin_t, *, rope, with_q):
    b, s, _ = proj.shape
    ts = min(512, s)
    n_q = N_Q_HEADS if with_q else 0
    in_specs, args = [], []
    if with_q:
        in_specs.append(BS((1, ts, ATTN_WIDTH), lambda bi, i: (bi, i, 0)))
        args.append(proj)
    in_specs.append(BS((1, ts, 2 * KV_WIDTH), lambda bi, i: (bi, i, ATTN_WIDTH // (2 * KV_WIDTH))))
    args.append(proj)
    gain = BS((1, HEAD_DIM), lambda bi, i: (0, 0))
    in_specs += [gain, gain]
    args += [q_gain, k_gain]
    if rope:
        tab = BS((ts, HEAD_DIM), lambda bi, i: (i, 0))
        in_specs += [tab, tab]
        args += [cos_t, sin_t]
    out_shape, out_specs = [], []
    if with_q:
        out_shape.append(SDS((b, s, ATTN_WIDTH), BF16))
        out_specs.append(BS((1, ts, ATTN_WIDTH), lambda bi, i: (bi, i, 0)))
    for _ in range(2):
        out_shape.append(SDS((b, s, KV_WIDTH), BF16))
        out_specs.append(BS((1, ts, KV_WIDTH), lambda bi, i: (bi, i, 0)))
    res = pl.pallas_call(
        functools.partial(_qkv_kernel, rope=rope, n_q_heads=n_q, n_k_heads=N_KV_HEADS),
        out_shape=out_shape,
        grid=(b, s // ts),
        in_specs=in_specs,
        out_specs=out_specs,
        compiler_params=_params(("parallel", "parallel")),
        name="qkv_prepare",
    )(*args)
    return res if with_q else (None,) + tuple(res)


def _attn_kernel(q_ref, k_ref, vt_ref, o_ref, s_ref, *, tk):
    tq = q_ref.shape[1]
    skv = k_ref.shape[1]
    cols = Q_PER_KV * tq
    n = skv // tk
    qs = jnp.concatenate([q_ref[0, :, h * HEAD_DIM:(h + 1) * HEAD_DIM] for h in range(Q_PER_KV)], axis=0)

    def scores(c, slot):
        start = pl.multiple_of(c * tk, tk)
        s_ref[slot] = lax.dot_general(k_ref[0, pl.ds(start, tk), :], qs, (((1,), (1,)), ((), ())),
                                      preferred_element_type=F32)

    def consume(c, slot, m, acc):
        start = pl.multiple_of(c * tk, tk)
        vt = vt_ref[0, 0, :, pl.ds(start, tk)]
        m_new = jnp.maximum(m, jnp.max(s_ref[slot], axis=0, keepdims=True))
        p = jnp.exp2(s_ref[slot] - m_new).astype(BF16)
        alpha = jnp.exp2(m - m_new)
        return m_new, alpha * acc + jnp.dot(vt, p, preferred_element_type=F32)

    def pair(i, carry):
        m, acc = carry
        c = 2 * i
        scores(c + 1, 1)
        m, acc = consume(c, 0, m, acc)
        scores(c + 2, 0)
        return consume(c + 1, 1, m, acc)

    scores(0, 0)
    init = (jnp.full((1, cols), -jnp.inf, F32), jnp.zeros((VT_ROWS, cols), F32))
    m, acc = lax.fori_loop(0, n // 2, pair, init)
    _, acc = consume(n - 1, 0, m, acc)
    out = (acc[:HEAD_DIM] / acc[HEAD_DIM:HEAD_DIM + 1]).T
    for h in range(Q_PER_KV):
        o_ref[0, :, h * HEAD_DIM:(h + 1) * HEAD_DIM] = out[h * tq:(h + 1) * tq].astype(o_ref.dtype)


def attention(q, k, v):
    b, s, _ = q.shape
    skv = k.shape[1]
    tq = min(ATT_Q_ROWS, s)
    tk = ATT_K_ROWS if skv % ATT_K_ROWS == 0 else skv
    n = skv // tk
    assert skv % tk == 0 and s % tq == 0 and n % 2 == 1
    gw = Q_PER_KV * HEAD_DIM
    vt = jnp.swapaxes(v, 1, 2).reshape(b, N_KV_HEADS, HEAD_DIM, skv)
    vt = jnp.concatenate([vt, jnp.ones((b, N_KV_HEADS, VT_ROWS - HEAD_DIM, skv), BF16)], axis=2)
    return pl.pallas_call(
        functools.partial(_attn_kernel, tk=tk),
        out_shape=SDS((b, s, ATTN_WIDTH), BF16),
        grid=(b, N_KV_HEADS, s // tq),
        in_specs=[BS((1, tq, gw), lambda bi, h, i: (bi, i, h)),
                  BS((1, skv, HEAD_DIM), lambda bi, h, i: (bi, 0, h)),
                  BS((1, 1, VT_ROWS, skv), lambda bi, h, i: (bi, h, 0, 0))],
        out_specs=BS((1, tq, gw), lambda bi, h, i: (bi, i, h)),
        scratch_shapes=[pltpu.VMEM((2, tk, Q_PER_KV * tq), F32)],
        compiler_params=_params(("parallel", "parallel", "parallel")),
        name="attention",
    )(q, k, vt)


def _conv3_kernel(u_ref, w_ref, b_ref, o_ref):
    x = u_ref[0]
    s = x.shape[0]
    row = lax.broadcasted_iota(I32, x.shape, 0)
    prev = jnp.where(row == 0, 0.0, pltpu.roll(x, 1, 0))
    nxt = jnp.where(row == s - 1, 0.0, pltpu.roll(x, s - 1, 0))
    o_ref[0, 0] = prev * w_ref[0:1, :] + x * w_ref[1:2, :] + nxt * w_ref[2:3, :] + b_ref[...]


def hyena_short_conv(proj, conv_w, conv_b):
    b, s, _ = proj.shape
    cblocks = HYENA_WIDTH // LANES
    return pl.pallas_call(
        _conv3_kernel,
        out_shape=SDS((3, b, s, HYENA_WIDTH), F32),
        grid=(b, 3 * cblocks),
        in_specs=[BS((1, s, LANES), lambda bi, c: (bi, 0, U_OFFSET // LANES + c)),
                  BS((3, LANES), lambda bi, c: (0, c)),
                  BS((1, LANES), lambda bi, c: (0, c))],
        out_specs=BS((1, 1, s, LANES), lambda bi, c: (c // cblocks, bi, 0, c % cblocks)),
        compiler_params=_params(("parallel", "parallel")),
        name="hyena_short_conv",
    )(proj, conv_w, conv_b.reshape(1, -1))


def _filter_kernel(z_ref, w1_ref, b1_ref, w2_ref, b2_ref, w3_ref, fr_ref, dl_ref, k_ref, l1_ref, *, seq_len):
    i = pl.program_id(0)
    rows = z_ref.shape[0]
    z = z_ref[...]
    fr = fr_ref[...]
    a = jnp.sin(fr * (_dot3(z, w1_ref[0], w1_ref[1]) + b1_ref[...]))
    a = jnp.sin(fr * (_dot3(a, w2_ref[0], w2_ref[1]) + b2_ref[...]))
    h = _dot3(a, w3_ref[0], w3_ref[1])
    decay = jnp.exp(-z[:, 0:1] * dl_ref[...])
    n = i * rows + lax.broadcasted_iota(I32, (rows, 1), 0)
    keep = jnp.where(n == seq_len, 0.0, 1.0)

    @pl.when(i == 0)
    def _():
        l1_ref[...] = jnp.zeros_like(l1_ref)

    for o in range(2):
        ko = h[:, o * HYENA_WIDTH:(o + 1) * HYENA_WIDTH] * decay * keep
        k_ref[o] = ko
        l1_ref[o:o + 1, :] += jnp.sum(jnp.abs(ko), axis=0, keepdims=True)


def hyena_filter(seq_len, f_w1, f_b1, f_w2, f_b2, f_w3, f_freq):
    n2 = 2 * seq_len
    n = jnp.arange(n2, dtype=I32)
    lag = jnp.where(n < seq_len, n, n2 - n).astype(F32)
    t01 = lag / max(seq_len - 1, 1)
    bands = (HYENA_EMB_DIM - 1) // 2
    f = jnp.linspace(1e-4, bands - 1, bands, dtype=F32)
    fw = (2.0 * math.pi * lag / seq_len)[:, None] * f[None, :]
    z = jnp.concatenate([t01[:, None], jnp.cos(fw), -jnp.sin(fw),
                         jnp.zeros((n2, LANES - HYENA_EMB_DIM), F32)], -1)
    w1p = jnp.concatenate([f_w1, jnp.zeros((LANES - HYENA_EMB_DIM, HYENA_FILTER_HIDDEN), F32)], 0)
    max_decay = math.log(HYENA_TARGET) / HYENA_FAST_DECAY_PCT
    min_decay = math.log(HYENA_TARGET) / HYENA_SLOW_DECAY_PCT
    deltas = jnp.abs(jnp.linspace(min_decay, max_decay, HYENA_WIDTH, dtype=F32)).reshape(1, -1)
    rows = min(512, seq_len)
    half = seq_len // rows
    hid = HYENA_FILTER_HIDDEN
    full = lambda shp: BS(shp, lambda i: (0,) * len(shp))
    return pl.pallas_call(
        functools.partial(_filter_kernel, seq_len=seq_len),
        out_shape=[SDS((2, n2, HYENA_WIDTH), F32), SDS((SUBLANES, HYENA_WIDTH), F32)],
        grid=(n2 // rows,),
        in_specs=[BS((rows, LANES), lambda i: (i, 0)), full((2, LANES, hid)), full((1, hid)), full((2, hid, hid)),
                  full((1, hid)), BS((2, hid, 2 * HYENA_WIDTH), lambda i: (0, 0, i // half)), full((1, hid)),
                  full((1, HYENA_WIDTH))],
        out_specs=[BS((2, rows, HYENA_WIDTH), lambda i: (0, i, 0)), full((SUBLANES, HYENA_WIDTH))],
        compiler_params=_params(("arbitrary",)),
        name="hyena_filter",
    )(z, jnp.stack(_split(w1p)), f_b1.reshape(1, -1), jnp.stack(_split(f_w2)), f_b2.reshape(1, -1),
      jnp.stack(_split(f_w3)), f_freq.reshape(1, -1), deltas)


def _pack(re, im):
    hi = lax.bitcast_convert_type(re.astype(BF16).astype(F32), U32)
    lo = lax.bitcast_convert_type(im.astype(BF16).astype(F32), U32)
    return hi | (lo >> 16)


def _unpack(word):
    re = lax.bitcast_convert_type(word & jnp.uint32(0xFFFF0000), F32)
    im = lax.bitcast_convert_type(word << 16, F32)
    return re, im


def _fft_in_kernel(x_ref, m_ref, o_ref):
    r = FFT_RADIX
    for s in range(FFT_GROUP):
        a = jnp.dot(m_ref[s], x_ref[:, s, :].astype(BF16), preferred_element_type=F32)
        o_ref[:, s, :] = _pack(a[:r], a[r:])


def fft_stage_in(x3, mats):
    r = FFT_RADIX
    c = x3.shape[-1]
    g = FFT_GROUP
    return pl.pallas_call(
        _fft_in_kernel,
        out_shape=SDS((r, r, c), U32),
        grid=(r // g,),
        in_specs=[BS((r, g, c), lambda i: (0, i, 0)), BS((g, 2 * r, r), lambda i: (i, 0, 0))],
        out_specs=BS((r, g, c), lambda i: (0, i, 0)),
        compiler_params=_params(("parallel",)),
        name="fft_stage_in",
    )(x3, mats)


def _fft_mid_kernel(*refs, spectrum_only):
    if spectrum_only:
        a_ref, f_ref, sc_ref, o_ref = refs
    else:
        a_ref, h_ref, f_ref, ft_ref, o_ref = refs
    r = FFT_RADIX
    for s in range(a_ref.shape[0]):
        ar, ai = _unpack(a_ref[s])
        x = jnp.dot(f_ref[...], jnp.concatenate([ar, ai], axis=0).astype(BF16), preferred_element_type=F32)
        xr, xi = x[:r], x[r:]
        if spectrum_only:
            o_ref[s] = _pack(xr * sc_ref[...], xi * sc_ref[...])
        else:
            hr, hi = _unpack(h_ref[s])
            yr = xr * hr - xi * hi
            yi = xr * hi + xi * hr
            bmat = jnp.dot(ft_ref[...], jnp.concatenate([yr, yi], axis=0).astype(BF16),
                           preferred_element_type=F32)
            o_ref[s] = _pack(bmat[:r], bmat[r:])


def fft_stage_mid(a_pk, fmat, fmat_t, h_pk=None, scale=None):
    r = FFT_RADIX
    c = a_pk.shape[-1]
    kb = 4
    blk = BS((kb, r, c), lambda i: (i, 0, 0))
    mat = BS((2 * r, 2 * r), lambda i: (0, 0))
    spectrum_only = h_pk is None
    if spectrum_only:
        in_specs, args = [blk, mat, BS((1, c), lambda i: (0, 0))], (a_pk, fmat, scale)
    else:
        in_specs, args = [blk, blk, mat, mat], (a_pk, h_pk, fmat, fmat_t)
    return pl.pallas_call(
        functools.partial(_fft_mid_kernel, spectrum_only=spectrum_only),
        out_shape=SDS((r, r, c), U32),
        grid=(r // kb,),
        in_specs=in_specs,
        out_specs=blk,
        compiler_params=_params(("parallel",)),
        name="fft_stage_mid",
    )(*args)


def _fft_out_kernel(b_ref, m_ref, v_ref, g_ref, d_ref, o_ref):
    for s in range(FFT_GROUP):
        br, bi = _unpack(b_ref[:, s, :])
        y = jnp.dot(m_ref[s], jnp.concatenate([br, bi], axis=0).astype(BF16), preferred_element_type=F32)
        o_ref[:, s, :] = (g_ref[:, s, :] * (y + v_ref[:, s, :] * d_ref[...])).astype(o_ref.dtype)


def fft_stage_out(b_pk, mats_t, v3, gate3, skip, out_dtype):
    r = FFT_RADIX
    c = b_pk.shape[-1]
    g = FFT_GROUP
    blk = BS((r, g, c), lambda i: (0, i, 0))
    return pl.pallas_call(
        _fft_out_kernel,
        out_shape=SDS((r, r, c), out_dtype),
        grid=(r // g,),
        in_specs=[blk, BS((g, r, 2 * r), lambda i: (i, 0, 0)), blk, blk, BS((1, c), lambda i: (0, 0))],
        out_specs=blk,
        compiler_params=_params(("parallel",)),
        name="fft_stage_out",
    )(b_pk, mats_t, v3, gate3, skip)


def _dft_tables(full_input):
    r = FFT_RADIX
    n = r * r
    t1 = jnp.arange(r, dtype=I32)[:, None, None]
    k2 = jnp.arange(r, dtype=I32)[None, :, None]
    nt2 = r if full_input else r // 2
    t2 = jnp.arange(nt2, dtype=I32)[None, None, :]
    ang = (2.0 * math.pi / n) * (((r * t2 + t1) * k2) % n).astype(F32)
    c, s = jnp.cos(ang), jnp.sin(ang)
    if full_input:
        return jnp.concatenate([c, -s], axis=1).astype(BF16)
    top = jnp.concatenate([c, s], axis=2)
    bot = jnp.concatenate([-s, c], axis=2)
    mats = jnp.concatenate([top, bot], axis=1)
    mats_t = jnp.swapaxes(mats, 1, 2) * (1.0 / n)
    return mats.astype(BF16), mats_t.astype(BF16)


def _dft_radix():
    r = FFT_RADIX
    a = jnp.arange(r, dtype=I32)
    ang = (2.0 * math.pi / r) * ((a[:, None] * a[None, :]) % r).astype(F32)
    c, s = jnp.cos(ang), jnp.sin(ang)
    f = jnp.concatenate([jnp.concatenate([c, s], 1), jnp.concatenate([-s, c], 1)], 0)
    return f.astype(BF16), f.T.astype(BF16)


def _dense_dft(n, n_in, full_input):
    k = jnp.arange(n, dtype=I32)[:, None]
    t = jnp.arange(n_in, dtype=I32)[None, :]
    ang = (2.0 * math.pi / n) * ((k * t) % n).astype(F32)
    c, s = jnp.cos(ang), jnp.sin(ang)
    if full_input:
        return jnp.concatenate([c, -s], 0).astype(BF16)
    fwd = jnp.concatenate([jnp.concatenate([c, s], 1), jnp.concatenate([-s, c], 1)], 0)
    return fwd.astype(BF16), (fwd.T * (1.0 / n)).astype(BF16)


def hyena_long(vx, filt, l1, skip):
    _, b, seq, c = vx.shape
    assert b == 2
    r = FFT_RADIX
    inv_l1 = 1.0 / l1[:2]
    if 2 * seq == r * r:
        mats, mats_t = _dft_tables(False)
        mats_f = _dft_tables(True)
        fmat, fmat_t = _dft_radix()
        v3, x1, x2 = (vx[i].reshape(r, r, c) for i in range(3))
        spec = [fft_stage_mid(fft_stage_in(filt[o].reshape(r, r, c), mats_f), fmat, fmat_t,
                              scale=inv_l1[o:o + 1]) for o in range(2)]
        z3 = fft_stage_out(fft_stage_mid(fft_stage_in(v3, mats), fmat, fmat_t, h_pk=spec[0]),
                           mats_t, v3, x1, skip[0:1], F32)
        y3 = fft_stage_out(fft_stage_mid(fft_stage_in(z3, mats), fmat, fmat_t, h_pk=spec[1]),
                           mats_t, z3, x2, skip[1:2], BF16)
        return y3.reshape(b, seq, c)
    n = 2 * seq
    fwd, inv = _dense_dft(n, seq, False)
    fwd_f = _dense_dft(n, n, True)
    v, x1, x2 = (vx[i].reshape(b * seq, c) for i in range(3))

    def conv(u, o):
        hs = matmul([fwd_f], filt[o][None], 0, n_out=c) * inv_l1[o:o + 1]
        xs = matmul([fwd], u[None], 0, n_out=c)
        hr, hi, xr, xi = hs[:n], hs[n:], xs[:n], xs[n:]
        ys = jnp.concatenate([xr * hr - xi * hi, xr * hi + xi * hr], 0)
        return matmul([inv], ys[None], 0, n_out=c)

    z = x1 * (conv(v, 0) + v * skip[0:1])
    y = x2 * (conv(z, 1) + z * skip[1:2])
    return y.astype(BF16).reshape(b, seq, c)


def _dwconv_kernel(x_ref, p_ref, n_ref, w_ref, b_ref, g_ref, be_ref, o_ref, win_ref, y_ref, sh_ref):
    i = pl.program_id(1)
    nt = pl.num_programs(1)
    ts = x_ref.shape[1]
    d = x_ref.shape[2]
    win_ref[0:HALO, :] = jnp.where(i == 0, 0.0, p_ref[0])
    win_ref[HALO:HALO + ts, :] = x_ref[0]
    win_ref[HALO + ts:, :] = jnp.where(i == nt - 1, 0.0, n_ref[0])
    pad = (CONF_K - 1) // 2
    first = HALO - pad
    rows = DW_ROWS
    def lane_block(cb, carry):
        cs = pl.ds(pl.multiple_of(cb * LANES, LANES), LANES)
        wb = [jnp.broadcast_to(w_ref[j:j + 1, cs], (SUBLANES, LANES)) for j in range(CONF_K)]
        bb = jnp.broadcast_to(b_ref[:, cs], (SUBLANES, LANES))
        for r0 in range(0, ts, rows):
            acc = jnp.zeros((rows // SUBLANES, SUBLANES, LANES), F32)
            for r in range(SUBLANES):
                offs = [o for o in range(first, first + CONF_K) if o % SUBLANES == r]
                span = rows + (max(offs) // SUBLANES) * SUBLANES
                sh_ref[r, 0:span, :] = win_ref[pl.ds(r0 + r, span), cs]
                for o in offs:
                    a8 = (o // SUBLANES) * SUBLANES
                    tap = sh_ref[r, a8:a8 + rows, :].reshape(rows // SUBLANES, SUBLANES, LANES)
                    acc = acc + tap * wb[o - first]
            y_ref[r0:r0 + rows, cs] = (acc + bb).reshape(rows, LANES)
        return carry

    lax.fori_loop(0, d // LANES, lane_block, 0)
    y = y_ref[...]
    mu = jnp.mean(y, axis=-1, keepdims=True)
    yc = y - mu
    var = jnp.mean(yc * yc, axis=-1, keepdims=True)
    a = yc * lax.rsqrt(var + NORM_EPS) * g_ref[...] + be_ref[...]
    o_ref[0] = (a * jax.nn.sigmoid(a)).astype(o_ref.dtype)


def conformer_dwconv(x, w, b, g, be):
    bsz, s, d = x.shape
    ts = LN_ROWS
    hb = ts // HALO
    nhb = s // HALO
    vec = lambda r: BS((r, d), lambda bi, i: (0, 0))
    return pl.pallas_call(
        _dwconv_kernel,
        out_shape=SDS((bsz, s, d), BF16),
        grid=(bsz, s // ts),
        in_specs=[BS((1, ts, d), lambda bi, i: (bi, i, 0)),
                  BS((1, HALO, d), lambda bi, i: (bi, jnp.maximum(i * hb - 1, 0), 0)),
                  BS((1, HALO, d), lambda bi, i: (bi, jnp.minimum((i + 1) * hb, nhb - 1), 0)),
                  vec(CONF_K), vec(1), vec(1), vec(1)],
        out_specs=BS((1, ts, d), lambda bi, i: (bi, i, 0)),
        scratch_shapes=[pltpu.VMEM((ts + 2 * HALO, d), F32), pltpu.VMEM((ts, d), F32),
                        pltpu.VMEM((SUBLANES, DW_ROWS + 2 * HALO, LANES), F32)],
        compiler_params=_params(("parallel", "parallel")),
        name="conformer_dwconv",
    )(x, x, x, w, b.reshape(1, d), g.reshape(1, d), be.reshape(1, d))


def _layer_norm(v, g, b):
    mu = jnp.mean(v, axis=-1, keepdims=True)
    vc = v - mu
    var = jnp.mean(vc * vc, axis=-1, keepdims=True)
    return vc * lax.rsqrt(var + NORM_EPS) * g + b


def _route(tok, rw_ref, rb_ref):
    rows = tok.shape[0]
    logits_t = _dot3(tok, rw_ref[0], rw_ref[1]).T
    aff = jax.nn.sigmoid(logits_t[:N_EXPERTS])
    sel = aff + rb_ref[:N_EXPERTS]
    sc = [sel[e:e + 1] for e in range(N_EXPERTS)]
    ac = [aff[e:e + 1] for e in range(N_EXPERTS)]
    scores = []
    for gi in range(N_GROUPS):
        m = sc[gi * 4:(gi + 1) * 4]
        best = m[0] + m[1]
        for a in range(4):
            for bb in range(a + 1, 4):
                if (a, bb) != (0, 1):
                    best = jnp.maximum(best, m[a] + m[bb])
        scores.append(best)
    top = scores[0]
    grp = jnp.zeros((1, rows), I32)
    for gi in range(1, N_GROUPS):
        upd = scores[gi] > top
        grp = jnp.where(upd, gi, grp)
        top = jnp.where(upd, scores[gi], top)

    def pick(cols, k):
        out = cols[k]
        for gi in range(1, N_GROUPS):
            out = jnp.where(grp == gi, cols[gi * 4 + k], out)
        return out

    ms = [pick(sc, k) for k in range(4)]
    ma = [pick(ac, k) for k in range(4)]

    def argmax4(vals):
        bv, bi, ba = vals[0], jnp.zeros((1, rows), I32), ma[0]
        for k in range(1, 4):
            upd = vals[k] > bv
            bi = jnp.where(upd, k, bi)
            ba = jnp.where(upd, ma[k], ba)
            bv = jnp.where(upd, vals[k], bv)
        return bi, ba

    i1, a1 = argmax4(ms)
    i2, a2 = argmax4([jnp.where(i1 == k, -jnp.inf, ms[k]) for k in range(4)])
    tot = a1 + a2
    fields = [(grp * 4 + i1).astype(F32), (grp * 4 + i2).astype(F32), a1 / tot, a2 / tot]
    route_t = jnp.concatenate(fields + [jnp.zeros((LANES - len(fields), rows), F32)], axis=0)
    return route_t.T


def _ln_router_kernel(x_ref, y_ref, g_ref, lg_ref, lb_ref, sc_ref, sh_ref, rw_ref, rb_ref,
                      xo_ref, tok_ref, rt_ref):
    v = DEEPNORM_ALPHA * x_ref[0] + g_ref[0] * y_ref[0].astype(F32)
    xn = _layer_norm(v, lg_ref[...], lb_ref[...])
    xo_ref[0] = xn
    tok = xn * (1.0 + sc_ref[0]) + sh_ref[0]
    half = tok.shape[1] // 2
    tok_ref[0] = _pack(tok[:, :half], tok[:, half:])
    rt_ref[0] = _route(tok, rw_ref, rb_ref)


def ln_router(x, y, gate, ln_g, ln_b, sc, sh, rw_pad, rb_pad):
    b, s, d = x.shape
    ts = min(LN_ROWS, s)
    tile = BS((1, ts, d), lambda bi, i: (bi, i, 0))
    vec = BS((1, 1, d), lambda bi, i: (bi, 0, 0))
    par = BS((1, d), lambda bi, i: (0, 0))
    return pl.pallas_call(
        _ln_router_kernel,
        out_shape=[SDS((b, s, d), F32), SDS((b, s, d // 2), U32), SDS((b, s, LANES), F32)],
        grid=(b, s // ts),
        in_specs=[tile, tile, vec, par, par, vec, vec,
                  BS((2, d, LANES), lambda bi, i: (0, 0, 0)), BS((LANES, ts), lambda bi, i: (0, 0))],
        out_specs=[tile, BS((1, ts, d // 2), lambda bi, i: (bi, i, 0)), BS((1, ts, LANES), lambda bi, i: (bi, i, 0))],
        compiler_params=_params(("parallel", "parallel")),
        name="ln_router",
    )(x, y.reshape(b, s, d), gate, ln_g.reshape(1, d), ln_b.reshape(1, d), sc, sh, rw_pad, rb_pad)


def _row_copy_wait(src_rows_ref, dst_any, sem, n_rows):
    pltpu.make_async_copy(src_rows_ref, dst_any.at[pl.ds(0, n_rows), :], sem).wait()


def _dispatch_kernel(pos_ref, lt_ref, na_ref, lat_ref, ctx_ref, xs_hbm, zero_ref, sem, zsem, *, n_lat, n_tok):
    i = pl.program_id(0)
    ts = lat_ref.shape[0]
    n_tiles = xs_hbm.shape[0] // GMM_ROWS

    def zero_tile(t):
        start = pl.multiple_of(t * GMM_ROWS, GMM_ROWS)
        return pltpu.make_async_copy(zero_ref, xs_hbm.at[pl.ds(start, GMM_ROWS), :], zsem)

    @pl.when(i == 0)
    def _():
        zero_ref[...] = jnp.zeros_like(zero_ref)
        for e in range(N_EXPERTS):
            @pl.when(lt_ref[e] >= 0)
            def _():
                zero_tile(lt_ref[e]).start()

        def start_tail(t, c):
            zero_tile(t).start()
            return c

        lax.fori_loop(na_ref[0], n_tiles, start_tail, 0)
        for e in range(N_EXPERTS):
            @pl.when(lt_ref[e] >= 0)
            def _():
                zero_tile(0).wait()

        def wait_tail(t, c):
            zero_tile(0).wait()
            return c

        lax.fori_loop(na_ref[0], n_tiles, wait_tail, 0)

    def scatter(src_ref):
        base = i * ts

        def body(r, c):
            for k in range(2):
                dst = pos_ref[k * n_tok + base + r]
                pltpu.make_async_copy(src_ref.at[pl.ds(r, 1), :], xs_hbm.at[pl.ds(dst, 1), :], sem).start()
            return c

        lax.fori_loop(0, ts, body, 0, unroll=ROW_COPY_UNROLL)
        for _ in range(2):
            _row_copy_wait(src_ref, xs_hbm, sem, ts)

    if ctx_ref is None:
        scatter(lat_ref)
    else:
        @pl.when(i < n_lat)
        def _():
            scatter(lat_ref)

        @pl.when(i >= n_lat)
        def _():
            scatter(ctx_ref)


def dispatch(pos, last_tile, n_active, tok_lat, tok_ctx, n_rows):
    t_lat, d = tok_lat.shape
    ts = LN_ROWS
    n_lat = t_lat // ts
    n_ctx = 0 if tok_ctx is None else tok_ctx.shape[0] // ts
    n_tok = t_lat + (0 if tok_ctx is None else tok_ctx.shape[0])
    in_specs = [BS((ts, d), lambda i, p, l, na: (jnp.minimum(i, n_lat - 1), 0))]
    args = [tok_lat]
    if tok_ctx is not None:
        in_specs.append(BS((ts, d), lambda i, p, l, na: (jnp.maximum(i - n_lat, 0), 0)))
        args.append(tok_ctx)

    def kern(pos_ref, lt_ref, na_ref, *refs):
        if tok_ctx is None:
            lat_ref, xs_hbm, zero_ref, sem, zsem = refs
            ctx_ref = None
        else:
            lat_ref, ctx_ref, xs_hbm, zero_ref, sem, zsem = refs
        _dispatch_kernel(pos_ref, lt_ref, na_ref, lat_ref, ctx_ref, xs_hbm, zero_ref, sem, zsem,
                         n_lat=n_lat, n_tok=n_tok)

    return pl.pallas_call(
        kern,
        out_shape=SDS((n_rows, d), tok_lat.dtype),
        grid_spec=pltpu.PrefetchScalarGridSpec(
            num_scalar_prefetch=3, grid=(n_lat + n_ctx,),
            in_specs=in_specs,
            out_specs=BS(memory_space=pl.ANY),
            scratch_shapes=[pltpu.VMEM((GMM_ROWS, d), tok_lat.dtype), pltpu.SemaphoreType.DMA(()),
                            pltpu.SemaphoreType.DMA(())]),
        compiler_params=_params(("arbitrary",)),
        name="dispatch",
    )(pos, last_tile, n_active, *args)


def _gmm_kernel(te_ref, na_ref, x_ref, w1_ref, w3_ref, w2_ref, o_ref, acc_ref):
    i = pl.program_id(0)
    j = pl.program_id(1)
    dh = x_ref.shape[1]

    def half_mlp():
        xa, xb = _unpack(x_ref[...])
        xa = xa.astype(BF16)
        xb = xb.astype(BF16)

        def up(w_ref):
            return (jnp.dot(xa, w_ref[0, 0, :dh, :].astype(BF16), preferred_element_type=F32)
                    + jnp.dot(xb, w_ref[0, 0, dh:, :].astype(BF16), preferred_element_type=F32))

        a = up(w1_ref)
        b = up(w3_ref)
        h = (a * jax.nn.sigmoid(a) * b).astype(BF16)
        return jnp.dot(h, w2_ref[0, 0].astype(BF16), preferred_element_type=F32)

    @pl.when(jnp.logical_and(i < na_ref[0], j == 0))
    def _():
        acc_ref[...] = half_mlp()

    @pl.when(jnp.logical_and(i < na_ref[0], j != 0))
    def _():
        tot = acc_ref[...] + half_mlp()
        o_ref[...] = _pack(tot[:, :dh], tot[:, dh:])

    @pl.when(jnp.logical_and(i >= na_ref[0], j == 0))
    def _():
        o_ref[...] = jnp.zeros_like(o_ref)


def grouped_mlp(tile_expert, n_active, xs, w1, w3, w2, layer):
    n_rows, dh = xs.shape
    d = 2 * dh
    n_tiles = n_rows // GMM_ROWS

    def tile(i, te, na):
        return jnp.minimum(i, na[0] - 1)

    def half(i, j, na):
        last = na[0] - 1
        return jnp.where(i <= last, jnp.where(i % 2 == 0, j, 1 - j), jnp.where(last % 2 == 0, 1, 0))

    def expert(i, te, na):
        return te[tile(i, te, na)]

    return pl.pallas_call(
        _gmm_kernel,
        out_shape=SDS((n_rows, dh), U32),
        grid_spec=pltpu.PrefetchScalarGridSpec(
            num_scalar_prefetch=2, grid=(n_tiles, 2),
            in_specs=[BS((GMM_ROWS, dh), lambda i, j, te, na: (tile(i, te, na), 0)),
                      BS((1, 1, d, GMM_HALF), lambda i, j, te, na: (layer, expert(i, te, na), 0, half(i, j, na))),
                      BS((1, 1, d, GMM_HALF), lambda i, j, te, na: (layer, expert(i, te, na), 0, half(i, j, na))),
                      BS((1, 1, GMM_HALF, d), lambda i, j, te, na: (layer, expert(i, te, na), half(i, j, na), 0))],
            out_specs=BS((GMM_ROWS, dh), lambda i, j, te, na: (i, 0)),
            scratch_shapes=[pltpu.VMEM((GMM_ROWS, d), F32)]),
        compiler_params=_params(("arbitrary", "arbitrary")),
        name="grouped_mlp",
    )(tile_expert, n_active, xs, w1, w3, w2)


def _combine_ln_kernel(pos_ref, x_ref, rt_ref, g_ref, lg_ref, lb_ref, sc_ref, sh_ref, ys_hbm,
                       xo_ref, h_ref, buf_ref, sem, *, tok_off, n_tok, tiles_per_batch, emit_h):
    ts = x_ref.shape[1]
    step = pl.program_id(0) * tiles_per_batch + pl.program_id(1)
    n_steps = pl.num_programs(0) * tiles_per_batch
    slot = step % 2

    def gather(tile, dst_slot):
        base = tok_off + tile * ts

        def body(r, c):
            for k in range(2):
                src = pos_ref[k * n_tok + base + r]
                pltpu.make_async_copy(ys_hbm.at[pl.ds(src, 1), :], buf_ref.at[dst_slot, k, pl.ds(r, 1), :],
                                      sem.at[dst_slot]).start()
            return c

        lax.fori_loop(0, ts, body, 0, unroll=ROW_COPY_UNROLL)

    @pl.when(step == 0)
    def _():
        gather(step, slot)

    @pl.when(step + 1 < n_steps)
    def _():
        gather(step + 1, 1 - slot)

    for k in range(2):
        pltpu.make_async_copy(ys_hbm.at[pl.ds(0, ts), :], buf_ref.at[slot, k], sem.at[slot]).wait()
    rt = rt_ref[0]
    halves = []
    for part in range(2):
        y0 = _unpack(buf_ref[slot, 0])[part]
        y1 = _unpack(buf_ref[slot, 1])[part]
        halves.append(rt[:, 2:3] * y0 + rt[:, 3:4] * y1)
    moe = jnp.concatenate(halves, axis=1)
    xn = _layer_norm(DEEPNORM_ALPHA * x_ref[0] + g_ref[0] * moe, lg_ref[...], lb_ref[...])
    xo_ref[0] = xn
    if emit_h:
        h_ref[0] = (xn * (1.0 + sc_ref[0]) + sh_ref[0]).astype(h_ref.dtype)
    else:
        h_ref[0] = jnp.zeros(h_ref.shape[1:], h_ref.dtype)


def combine_ln(pos, x, route, gate, ln_g, ln_b, sc, sh, ys, *, tok_off, n_tok, emit_h=True):
    b, s, d = x.shape
    ts = min(LN_ROWS, s)
    tpb = s // ts
    tile = BS((1, ts, d), lambda bi, i, p: (bi, i, 0))
    vec = BS((1, 1, d), lambda bi, i, p: (bi, 0, 0))
    par = BS((1, d), lambda bi, i, p: (0, 0))
    h_rows = ts if emit_h else SUBLANES
    return pl.pallas_call(
        functools.partial(_combine_ln_kernel, tok_off=tok_off, n_tok=n_tok, tiles_per_batch=tpb, emit_h=emit_h),
        out_shape=[SDS((b, s, d), F32), SDS((b, tpb * h_rows, d), BF16)],
        grid_spec=pltpu.PrefetchScalarGridSpec(
            num_scalar_prefetch=1, grid=(b, tpb),
            in_specs=[tile, BS((1, ts, LANES), lambda bi, i, p: (bi, i, 0)), vec, par, par, vec, vec,
                      BS(memory_space=pl.ANY)],
            out_specs=[tile, BS((1, h_rows, d), lambda bi, i, p: (bi, i, 0))],
            scratch_shapes=[pltpu.VMEM((2, 2, ts, d // 2), U32), pltpu.SemaphoreType.DMA((2,))]),
        compiler_params=_params(("arbitrary", "arbitrary")),
        name="combine_ln",
    )(pos, x, route, gate, ln_g.reshape(1, d), ln_b.reshape(1, d), sc, sh, ys)


def _rope_tables(n_tokens):
    rows = n_tokens // GRID_W
    row = jnp.repeat(jnp.arange(rows, dtype=F32), GRID_W)
    col = jnp.tile(jnp.arange(GRID_W, dtype=F32), rows)
    half = HEAD_DIM // 2
    inv = ROPE_THETA ** (-jnp.arange(0, half, 2, dtype=F32) / half)
    ang = jnp.concatenate([row[:, None] * inv, col[:, None] * inv], -1)
    cos, sin = jnp.cos(ang), jnp.sin(ang)
    return jnp.concatenate([cos, cos], -1), jnp.concatenate([-sin, sin], -1)


def _head_perm():
    base = jnp.concatenate([jnp.arange(0, HEAD_DIM, 2), jnp.arange(1, HEAD_DIM, 2)])
    n_heads = N_Q_HEADS + N_KV_HEADS
    return base, jnp.concatenate([h * HEAD_DIM + base for h in range(n_heads)])


def _even_mixer(h, hc, ctx_out, cos_t, sin_t, w_in, q_g, k_g, conv_w, conv_b,
                f_w1, f_b1, f_w2, f_b2, f_w3, f_freq, skip, w_out_all, li):
    b, s, d = h.shape
    lc = hc.shape[1]
    base, perm = _head_perm()
    n_perm = perm.shape[0]
    w_in_p = jnp.concatenate([w_in[:, :n_perm][:, perm], w_in[:, n_perm:]], axis=1)[None]
    qg = q_g[base].reshape(1, HEAD_DIM)
    kg = k_g[base].reshape(1, HEAD_DIM)
    n_in = w_in.shape[1]

    proj = matmul([h.reshape(b * s, d)], w_in_p, 0, n_out=n_in, tn=768).reshape(b, s, n_in)
    proj_c = matmul([hc.reshape(b * lc, d)], w_in_p, 0, n_out=n_in, tn=768).reshape(b, lc, n_in)
    q, k, v = qkv_prepare(proj, qg, kg, cos_t, sin_t, rope=True, with_q=True)
    qc, kc, vc = qkv_prepare(proj_c, qg, kg, None, None, rope=False, with_q=ctx_out)
    att = attention(q, jnp.concatenate([k, kc], axis=1), jnp.concatenate([v, vc], axis=1))

    def hyena(p, seq):
        vx = hyena_short_conv(p, conv_w, conv_b)
        filt, l1 = hyena_filter(seq, f_w1, f_b1, f_w2, f_b2, f_w3, f_freq)
        return hyena_long(vx, filt, l1, skip)

    hy = hyena(proj, s)
    y = matmul([att.reshape(b * s, ATTN_WIDTH), hy.reshape(b * s, HYENA_WIDTH)], w_out_all, li,
               n_out=d).reshape(b, s, d)
    if not ctx_out:
        return y, None
    att_c = attention(qc, kc, vc)
    hy_c = hyena(proj_c, lc)
    yc = matmul([att_c.reshape(b * lc, ATTN_WIDTH), hy_c.reshape(b * lc, HYENA_WIDTH)], w_out_all, li,
                n_out=d).reshape(b, lc, d)
    return y, yc


def _conformer(h, in_w, in_b, dw_w, dw_b, n_g, n_b, out_w, out_b, li):
    b, s, d = h.shape
    a = matmul([h.reshape(b * s, d)], in_w, li, n_out=d, bias=in_b, glu=True).reshape(b, s, d)
    a = conformer_dwconv(a, dw_w[li], dw_b[li], n_g[li], n_b[li])
    return matmul([a.reshape(b * s, d)], out_w, li, n_out=d, bias=out_b).reshape(b, s, d)


def _dispatch_plan(experts, n_tiles):
    n_tok = experts.shape[0]
    flat = experts.T.reshape(-1)
    hit = flat[:, None] == jnp.arange(N_EXPERTS, dtype=I32)[None, :]
    blk = LANES
    oh = hit.astype(F32).reshape(-1, blk, N_EXPERTS)
    strict_lower = (jnp.arange(blk)[:, None] > jnp.arange(blk)[None, :]).astype(F32)
    within = jnp.einsum("ij,bje->bie", strict_lower, oh, precision=HIGHEST)
    blk_tot = jnp.sum(oh, axis=1)
    blk_off = jnp.cumsum(blk_tot, axis=0) - blk_tot
    before = (within + blk_off[:, None, :]).reshape(-1, N_EXPERTS)
    rank = jnp.sum(jnp.where(hit, before, 0.0), axis=1).astype(I32)
    counts = jnp.sum(blk_tot, axis=0).astype(I32)
    padded = ((counts + GMM_ROWS - 1) // GMM_ROWS) * GMM_ROWS
    ends = jnp.cumsum(padded)
    starts = ends - padded
    pos = (starts[flat] + rank).astype(I32)
    n_active = (ends[-1] // GMM_ROWS).astype(I32).reshape(1)
    tile_start = jnp.arange(n_tiles, dtype=I32) * GMM_ROWS
    tile_expert = jnp.minimum(jnp.sum((ends[None, :] <= tile_start[:, None]).astype(I32), axis=1),
                              N_EXPERTS - 1).astype(I32)
    last_tile = jnp.where(padded > 0, ends // GMM_ROWS - 1, -1).astype(I32)
    del n_tok
    return pos, n_active, tile_expert, last_tile


def kernel(x, c, ctx, c_ctx, mod_w, mod_b, ln_g, ln_b, mix_in_w, q_norm_g, k_norm_g, hy_conv_w, hy_conv_b,
           hy_w1, hy_b1, hy_w2, hy_b2, hy_w3, hy_freq, hy_skip, mix_out_w, cf_in_w, cf_in_b, cf_dw_w, cf_dw_b,
           cf_ln_g, cf_ln_b, cf_out_w, cf_out_b, router_w, router_b, exp_w1, exp_w3, exp_w2):
    b, s, d = x.shape
    lc = ctx.shape[1]
    cos_t, sin_t = _rope_tables(s)
    cvec = jnp.concatenate([c, c_ctx[None], jnp.zeros((SUBLANES - b - 1, d), F32)], 0)
    mods = mod_vectors(cvec, mod_w, mod_b)
    rw_pad = jnp.stack(_split(jnp.concatenate([router_w, jnp.zeros((d, LANES - N_EXPERTS), F32)], 1)))
    rb_pad = jnp.broadcast_to(jnp.concatenate([router_b, jnp.zeros((LANES - N_EXPERTS,), F32)])[:, None],
                              (LANES, LN_ROWS))

    def mod_of(l, j, is_ctx):
        m = mods[l, :, j * d:(j + 1) * d]
        rows = jnp.broadcast_to(m[b:b + 1], (b, d)) if is_ctx else m[:b]
        return rows.reshape(b, 1, d)

    h = modulate(x, mod_of(0, 1, False), mod_of(0, 0, False))
    hc = modulate(ctx, mod_of(0, 1, True), mod_of(0, 0, True))
    for l in range(DEPTH):
        ctx_needed = any(j > l and j % 2 == 0 for j in range(DEPTH))
        li = l // 2
        if l % 2 == 0:
            y, yc = _even_mixer(h, hc, ctx_needed, cos_t, sin_t, mix_in_w[li], q_norm_g[li], k_norm_g[li],
                                hy_conv_w[li], hy_conv_b[li], hy_w1[li], hy_b1[li], hy_w2[li], hy_b2[li],
                                hy_w3[li], hy_freq[li], hy_skip[li], mix_out_w, li)
        else:
            cf = (cf_in_w, cf_in_b, cf_dw_w, cf_dw_b, cf_ln_g, cf_ln_b, cf_out_w, cf_out_b)
            y = _conformer(h, *cf, li)
            yc = _conformer(hc, *cf, li) if ctx_needed else None

        x, tok, route = ln_router(x, y, mod_of(l, 2, False), ln_g[l, 0], ln_b[l, 0],
                                  mod_of(l, 4, False), mod_of(l, 3, False), rw_pad, rb_pad)
        routes = [route.reshape(b * s, LANES)]
        tok_c = None
        if ctx_needed:
            ctx, tok_c, route_c = ln_router(ctx, yc, mod_of(l, 2, True), ln_g[l, 0], ln_b[l, 0],
                                            mod_of(l, 4, True), mod_of(l, 3, True), rw_pad, rb_pad)
            routes.append(route_c.reshape(b * lc, LANES))
            tok_c = tok_c.reshape(b * lc, d // 2)
        n_tok = b * s + (b * lc if ctx_needed else 0)
        n_tiles = (2 * n_tok) // GMM_ROWS + N_EXPERTS
        experts = jnp.concatenate([r[:, :2] for r in routes], 0).astype(I32)
        pos, n_active, tile_expert, last_tile = _dispatch_plan(experts, n_tiles)
        xs = dispatch(pos, last_tile, n_active, tok.reshape(b * s, d // 2), tok_c, n_tiles * GMM_ROWS)
        ys = grouped_mlp(tile_expert, n_active, xs, exp_w1, exp_w3, exp_w2, l)

        last = l == DEPTH - 1
        nl = min(l + 1, DEPTH - 1)
        x, h = combine_ln(pos, x, route, mod_of(l, 5, False), ln_g[l, 1], ln_b[l, 1],
                          mod_of(nl, 1, False), mod_of(nl, 0, False), ys, tok_off=0, n_tok=n_tok,
                          emit_h=not last)
        if ctx_needed:
            ctx, hc = combine_ln(pos, ctx, route_c, mod_of(l, 5, True), ln_g[l, 1], ln_b[l, 1],
                                 mod_of(nl, 1, True), mod_of(nl, 0, True), ys, tok_off=b * s, n_tok=n_tok)
    return x
```

```python
import functools
import math

import jax
import jax.numpy as jnp
from jax import lax
from jax.experimental import pallas as pl
from jax.experimental.pallas import tpu as pltpu

F32 = jnp.float32
BF16 = jnp.bfloat16
U32 = jnp.uint32
I32 = jnp.int32
HIGHEST = lax.Precision.HIGHEST
SDS = jax.ShapeDtypeStruct
BS = pl.BlockSpec

D_MODEL = 2048
DEPTH = 4
GRID_W = 64
HEAD_DIM = 128
N_Q_HEADS = 8
N_KV_HEADS = 2
Q_PER_KV = N_Q_HEADS // N_KV_HEADS
ATTN_WIDTH = N_Q_HEADS * HEAD_DIM
KV_WIDTH = N_KV_HEADS * HEAD_DIM
ATTN_SCALE = HEAD_DIM ** -0.5
ROPE_THETA = 10000.0
HYENA_WIDTH = D_MODEL - ATTN_WIDTH
HYENA_EMB_DIM = 33
HYENA_FILTER_HIDDEN = 64
HYENA_FAST_DECAY_PCT = 0.3
HYENA_SLOW_DECAY_PCT = 1.5
HYENA_TARGET = 1e-2
U_OFFSET = ATTN_WIDTH + 2 * KV_WIDTH
CONF_K = 31
N_EXPERTS = 16
N_GROUPS = 4
EXPERTS_PER_GROUP = 4
D_EXPERT = 1024
DEEPNORM_ALPHA = (2 * DEPTH) ** 0.25
N_MOD = 6
NORM_EPS = 1e-6
LOG2_E = math.log2(math.e)

LANES = 128
SUBLANES = 8
VMEM_LIMIT_BYTES = 56 * 1024 * 1024

LN_ROWS = 256
MM_ROWS = 1024
GMM_ROWS = 512
GMM_HALF = D_EXPERT // 2
FFT_RADIX = 128
FFT_GROUP = 8
ATT_Q_ROWS = 512
ATT_K_ROWS = 768
VT_ROWS = HEAD_DIM + 16
HALO = 16
DW_ROWS = 64
ROW_COPY_UNROLL = 8


def _params(semantics):
    return pltpu.CompilerParams(dimension_semantics=semantics, vmem_limit_bytes=VMEM_LIMIT_BYTES)


def _split(x):
    hi = x.astype(BF16)
    return hi, (x - hi.astype(F32)).astype(BF16)


def _dot3(x, w_hi, w_lo):
    x_hi, x_lo = _split(x)
    dot = functools.partial(jnp.dot, preferred_element_type=F32)
    return dot(x_hi, w_hi) + (dot(x_hi, w_lo) + dot(x_lo, w_hi))


def _mod_kernel(c_ref, w_ref, b_ref, o_ref):
    c = c_ref[...]
    s = c * jax.nn.sigmoid(c)
    o_ref[0] = jnp.dot(s, w_ref[0], precision=HIGHEST, preferred_element_type=F32) + b_ref[0]


def mod_vectors(cvec, mod_w, mod_b):
    depth, d, n = mod_w.shape
    tn = 1024
    return pl.pallas_call(
        _mod_kernel,
        out_shape=SDS((depth, SUBLANES, n), F32),
        grid=(depth, n // tn),
        in_specs=[BS((SUBLANES, d), lambda l, j: (0, 0)),
                  BS((1, d, tn), lambda l, j: (l, 0, j)),
                  BS((1, 1, tn), lambda l, j: (l, 0, j))],
        out_specs=BS((1, SUBLANES, tn), lambda l, j: (l, 0, j)),
        compiler_params=_params(("parallel", "parallel")),
        name="mod_vectors",
    )(cvec, mod_w, mod_b.reshape(depth, 1, n))


def _modulate_kernel(x_ref, sc_ref, sh_ref, o_ref):
    o_ref[0] = (x_ref[0] * (1.0 + sc_ref[0]) + sh_ref[0]).astype(o_ref.dtype)


def modulate(x, sc, sh):
    b, s, d = x.shape
    ts = min(LN_ROWS, s)
    vec = BS((1, 1, d), lambda bi, i: (bi, 0, 0))
    return pl.pallas_call(
        _modulate_kernel,
        out_shape=SDS((b, s, d), BF16),
        grid=(b, s // ts),
        in_specs=[BS((1, ts, d), lambda bi, i: (bi, i, 0)), vec, vec],
        out_specs=BS((1, ts, d), lambda bi, i: (bi, i, 0)),
        compiler_params=_params(("parallel", "parallel")),
        name="modulate",
    )(x, sc, sh)


def _mm_kernel(*refs, k_sizes, has_bias, glu):
    n_a = len(k_sizes)
    a_refs = refs[:n_a]
    pos = n_a
    w_refs = refs[pos:pos + (2 if glu else 1)]
    pos += len(w_refs)
    b_refs = ()
    if has_bias:
        b_refs = refs[pos:pos + len(w_refs)]
        pos += len(w_refs)
    o_ref = refs[pos]
    wbf_refs = refs[pos + 1:]

    @pl.when(pl.program_id(1) == 0)
    def _():
        for w_ref, wbf in zip(w_refs, wbf_refs):
            wbf[...] = w_ref[0].astype(BF16)

    def linear(which):
        acc = None
        off = 0
        for a_ref, k in zip(a_refs, k_sizes):
            part = jnp.dot(a_ref[...].astype(BF16), wbf_refs[which][off:off + k, :],
                           preferred_element_type=F32)
            acc = part if acc is None else acc + part
            off += k
        if has_bias:
            acc = acc + b_refs[which][0]
        return acc

    out = linear(0)
    if glu:
        out = out * jax.nn.sigmoid(linear(1))
    o_ref[...] = out.astype(o_ref.dtype)


def matmul(a_list, w, layer, *, n_out, col_off=0, bias=None, glu=False, out_dtype=F32, tn=512):
    m = a_list[0].shape[0]
    k_sizes = tuple(a.shape[1] for a in a_list)
    k_total = sum(k_sizes)
    assert w.shape[1] == k_total and n_out % tn == 0 and col_off % tn == 0
    tm = min(MM_ROWS, m)
    assert m % tm == 0
    nb = n_out // tn
    cb = col_off // tn
    in_specs = [BS((tm, k), lambda j, i: (i, 0)) for k in k_sizes]
    args = list(a_list)
    n_w = 2 if glu else 1
    for h in range(n_w):
        in_specs.append(BS((1, k_total, tn), lambda j, i, h=h: (layer, 0, cb + h * nb + j)))
        args.append(w)
    if bias is not None:
        bias3 = bias.reshape(bias.shape[0], 1, bias.shape[1])
        for h in range(n_w):
            in_specs.append(BS((1, 1, tn), lambda j, i, h=h: (layer, 0, cb + h * nb + j)))
            args.append(bias3)
    return pl.pallas_call(
        functools.partial(_mm_kernel, k_sizes=k_sizes, has_bias=bias is not None, glu=glu),
        out_shape=SDS((m, n_out), out_dtype),
        grid=(nb, m // tm),
        in_specs=in_specs,
        out_specs=BS((tm, tn), lambda j, i: (i, j)),
        scratch_shapes=[pltpu.VMEM((k_total, tn), BF16) for _ in range(n_w)],
        compiler_params=_params(("parallel", "arbitrary")),
        name="matmul_glu" if glu else "matmul",
    )(*args)


def _qkv_kernel(*refs, rope, n_q_heads, n_k_heads):
    it = iter(refs)
    q_ref = next(it) if n_q_heads else None
    kv_ref = next(it)
    qg_ref = next(it)
    kg_ref = next(it)
    cos_ref = sin_next_ref = sin_prev_ref = None
    if rope:
        cos_ref = next(it)
        sin_next_ref = next(it)
        sin_prev_ref = next(it)
    qo_ref = next(it) if n_q_heads else None
    ko_ref = next(it)
    vo_ref = next(it)

    def prep(xh, g, scale):
        ms = jnp.mean(xh * xh, axis=-1, keepdims=True)
        xn = xh * lax.rsqrt(ms + NORM_EPS) * g
        if rope:
            xn = (xn * cos_ref[...] + pltpu.roll(xn, HEAD_DIM - 1, 1) * sin_next_ref[...]
                  + pltpu.roll(xn, 1, 1) * sin_prev_ref[...])
        if scale != 1.0:
            xn = xn * scale
        return xn.astype(BF16)

    for h in range(n_q_heads):
        sl = slice(h * HEAD_DIM, (h + 1) * HEAD_DIM)
        qo_ref[0, :, sl] = prep(q_ref[0, :, sl], qg_ref[...], ATTN_SCALE * LOG2_E)
    for h in range(n_k_heads):
        sl = slice(h * HEAD_DIM, (h + 1) * HEAD_DIM)
        ko_ref[0, :, sl] = prep(kv_ref[0, :, sl], kg_ref[...], 1.0)
    vo_ref[0] = kv_ref[0, :, KV_WIDTH:].astype(BF16)


def qkv_prepare(proj, q_gain, k_gain, rope_tables, *, with_q):
    rope = rope_tables is not None
    b, s, _ = proj.shape
    ts = min(512, s)
    n_q = N_Q_HEADS if with_q else 0
    in_specs, args = [], []
    if with_q:
        in_specs.append(BS((1, ts, ATTN_WIDTH), lambda bi, i: (bi, i, 0)))
        args.append(proj)
    in_specs.append(BS((1, ts, 2 * KV_WIDTH), lambda bi, i: (bi, i, ATTN_WIDTH // (2 * KV_WIDTH))))
    args.append(proj)
    gain = BS((1, HEAD_DIM), lambda bi, i: (0, 0))
    in_specs += [gain, gain]
    args += [q_gain, k_gain]
    if rope:
        tab = BS((ts, HEAD_DIM), lambda bi, i: (i, 0))
        in_specs += [tab] * len(rope_tables)
        args += list(rope_tables)
    out_shape, out_specs = [], []
    if with_q:
        out_shape.append(SDS((b, s, ATTN_WIDTH), BF16))
        out_specs.append(BS((1, ts, ATTN_WIDTH), lambda bi, i: (bi, i, 0)))
    for _ in range(2):
        out_shape.append(SDS((b, s, KV_WIDTH), BF16))
        out_specs.append(BS((1, ts, KV_WIDTH), lambda bi, i: (bi, i, 0)))
    res = pl.pallas_call(
        functools.partial(_qkv_kernel, rope=rope, n_q_heads=n_q, n_k_heads=N_KV_HEADS),
        out_shape=out_shape,
        grid=(b, s // ts),
        in_specs=in_specs,
        out_specs=out_specs,
        compiler_params=_params(("parallel", "parallel")),
        name="qkv_prepare",
    )(*args)
    return res if with_q else (None,) + tuple(res)


def _attn_kernel(q_ref, k_ref, vt_ref, o_ref, s_ref, *, tk):
    tq = q_ref.shape[1]
    skv = k_ref.shape[1]
    cols = Q_PER_KV * tq
    n = skv // tk
    qs = jnp.concatenate([q_ref[0, :, h * HEAD_DIM:(h + 1) * HEAD_DIM] for h in range(Q_PER_KV)], axis=0)

    def scores(c, slot):
        start = pl.multiple_of(c * tk, tk)
        s_ref[slot] = lax.dot_general(k_ref[0, pl.ds(start, tk), :], qs, (((1,), (1,)), ((), ())),
                                      preferred_element_type=F32)

    def consume(c, slot, m, acc):
        start = pl.multiple_of(c * tk, tk)
        vt = vt_ref[0, 0, :, pl.ds(start, tk)]
        m_new = jnp.maximum(m, jnp.max(s_ref[slot], axis=0, keepdims=True))
        p = jnp.exp2(s_ref[slot] - m_new).astype(BF16)
        alpha = jnp.exp2(m - m_new)
        return m_new, alpha * acc + jnp.dot(vt, p, preferred_element_type=F32)

    def pair(i, carry):
        m, acc = carry
        c = 2 * i
        scores(c + 1, 1)
        m, acc = consume(c, 0, m, acc)
        scores(c + 2, 0)
        return consume(c + 1, 1, m, acc)

    scores(0, 0)
    init = (jnp.full((1, cols), -jnp.inf, F32), jnp.zeros((VT_ROWS, cols), F32))
    m, acc = lax.fori_loop(0, n // 2, pair, init)
    _, acc = consume(n - 1, 0, m, acc)
    out = (acc[:HEAD_DIM] / acc[HEAD_DIM:HEAD_DIM + 1]).T
    for h in range(Q_PER_KV):
        o_ref[0, :, h * HEAD_DIM:(h + 1) * HEAD_DIM] = out[h * tq:(h + 1) * tq].astype(o_ref.dtype)


def attention(q, k, v):
    b, s, _ = q.shape
    skv = k.shape[1]
    tq = min(ATT_Q_ROWS, s)
    tk = ATT_K_ROWS if skv % ATT_K_ROWS == 0 else skv
    n = skv // tk
    assert skv % tk == 0 and s % tq == 0 and n % 2 == 1
    gw = Q_PER_KV * HEAD_DIM
    vt = jnp.swapaxes(v, 1, 2).reshape(b, N_KV_HEADS, HEAD_DIM, skv)
    vt = jnp.concatenate([vt, jnp.ones((b, N_KV_HEADS, VT_ROWS - HEAD_DIM, skv), BF16)], axis=2)
    return pl.pallas_call(
        functools.partial(_attn_kernel, tk=tk),
        out_shape=SDS((b, s, ATTN_WIDTH), BF16),
        grid=(b, N_KV_HEADS, s // tq),
        in_specs=[BS((1, tq, gw), lambda bi, h, i: (bi, i, h)),
                  BS((1, skv, HEAD_DIM), lambda bi, h, i: (bi, 0, h)),
                  BS((1, 1, VT_ROWS, skv), lambda bi, h, i: (bi, h, 0, 0))],
        out_specs=BS((1, tq, gw), lambda bi, h, i: (bi, i, h)),
        scratch_shapes=[pltpu.VMEM((2, tk, Q_PER_KV * tq), F32)],
        compiler_params=_params(("parallel", "parallel", "parallel")),
        name="attention",
    )(q, k, vt)


def _conv3_kernel(u_ref, w_ref, b_ref, o_ref):
    x = u_ref[0]
    s = x.shape[0]
    row = lax.broadcasted_iota(I32, x.shape, 0)
    prev = jnp.where(row == 0, 0.0, pltpu.roll(x, 1, 0))
    nxt = jnp.where(row == s - 1, 0.0, pltpu.roll(x, s - 1, 0))
    o_ref[0, 0] = prev * w_ref[0:1, :] + x * w_ref[1:2, :] + nxt * w_ref[2:3, :] + b_ref[...]


def hyena_short_conv(proj, conv_w, conv_b):
    b, s, _ = proj.shape
    cblocks = HYENA_WIDTH // LANES
    return pl.pallas_call(
        _conv3_kernel,
        out_shape=SDS((3, b, s, HYENA_WIDTH), F32),
        grid=(b, 3 * cblocks),
        in_specs=[BS((1, s, LANES), lambda bi, c: (bi, 0, U_OFFSET // LANES + c)),
                  BS((3, LANES), lambda bi, c: (0, c)),
                  BS((1, LANES), lambda bi, c: (0, c))],
        out_specs=BS((1, 1, s, LANES), lambda bi, c: (c // cblocks, bi, 0, c % cblocks)),
        compiler_params=_params(("parallel", "parallel")),
        name="hyena_short_conv",
    )(proj, conv_w, conv_b.reshape(1, -1))


def _filter_kernel(z_ref, w1_ref, b1_ref, w2_ref, b2_ref, w3_ref, fr_ref, dl_ref, k_ref, l1_ref, *, seq_len):
    i = pl.program_id(0)
    rows = z_ref.shape[0]
    z = z_ref[...]
    fr = fr_ref[...]
    a = jnp.sin(fr * (_dot3(z, w1_ref[0], w1_ref[1]) + b1_ref[...]))
    a = jnp.sin(fr * (_dot3(a, w2_ref[0], w2_ref[1]) + b2_ref[...]))
    h = _dot3(a, w3_ref[0], w3_ref[1])
    decay = jnp.exp(-z[:, 0:1] * dl_ref[...])
    n = i * rows + lax.broadcasted_iota(I32, (rows, 1), 0)
    keep = jnp.where(n == seq_len, 0.0, 1.0)

    @pl.when(i == 0)
    def _():
        l1_ref[...] = jnp.zeros_like(l1_ref)

    for o in range(2):
        ko = h[:, o * HYENA_WIDTH:(o + 1) * HYENA_WIDTH] * decay * keep
        k_ref[o] = ko
        l1_ref[o:o + 1, :] += jnp.sum(jnp.abs(ko), axis=0, keepdims=True)


def hyena_filter(seq_len, f_w1, f_b1, f_w2, f_b2, f_w3, f_freq):
    n2 = 2 * seq_len
    n = jnp.arange(n2, dtype=I32)
    lag = jnp.where(n < seq_len, n, n2 - n).astype(F32)
    t01 = lag / max(seq_len - 1, 1)
    bands = (HYENA_EMB_DIM - 1) // 2
    f = jnp.linspace(1e-4, bands - 1, bands, dtype=F32)
    fw = (2.0 * math.pi * lag / seq_len)[:, None] * f[None, :]
    z = jnp.concatenate([t01[:, None], jnp.cos(fw), -jnp.sin(fw),
                         jnp.zeros((n2, LANES - HYENA_EMB_DIM), F32)], -1)
    w1p = jnp.concatenate([f_w1, jnp.zeros((LANES - HYENA_EMB_DIM, HYENA_FILTER_HIDDEN), F32)], 0)
    max_decay = math.log(HYENA_TARGET) / HYENA_FAST_DECAY_PCT
    min_decay = math.log(HYENA_TARGET) / HYENA_SLOW_DECAY_PCT
    deltas = jnp.abs(jnp.linspace(min_decay, max_decay, HYENA_WIDTH, dtype=F32)).reshape(1, -1)
    rows = min(512, seq_len)
    half = seq_len // rows
    hid = HYENA_FILTER_HIDDEN
    full = lambda shp: BS(shp, lambda i: (0,) * len(shp))
    return pl.pallas_call(
        functools.partial(_filter_kernel, seq_len=seq_len),
        out_shape=[SDS((2, n2, HYENA_WIDTH), F32), SDS((SUBLANES, HYENA_WIDTH), F32)],
        grid=(n2 // rows,),
        in_specs=[BS((rows, LANES), lambda i: (i, 0)), full((2, LANES, hid)), full((1, hid)), full((2, hid, hid)),
                  full((1, hid)), BS((2, hid, 2 * HYENA_WIDTH), lambda i: (0, 0, i // half)), full((1, hid)),
                  full((1, HYENA_WIDTH))],
        out_specs=[BS((2, rows, HYENA_WIDTH), lambda i: (0, i, 0)), full((SUBLANES, HYENA_WIDTH))],
        compiler_params=_params(("arbitrary",)),
        name="hyena_filter",
    )(z, jnp.stack(_split(w1p)), f_b1.reshape(1, -1), jnp.stack(_split(f_w2)), f_b2.reshape(1, -1),
      jnp.stack(_split(f_w3)), f_freq.reshape(1, -1), deltas)


def _pack(re, im):
    hi = lax.bitcast_convert_type(re.astype(BF16).astype(F32), U32)
    lo = lax.bitcast_convert_type(im.astype(BF16).astype(F32), U32)
    return hi | (lo >> 16)


def _unpack(word):
    re = lax.bitcast_convert_type(word & jnp.uint32(0xFFFF0000), F32)
    im = lax.bitcast_convert_type(word << 16, F32)
    return re, im


def _fft_in_kernel(x_ref, m_ref, o_ref):
    r = FFT_RADIX
    for s in range(FFT_GROUP):
        a = jnp.dot(m_ref[s], x_ref[:, s, :].astype(BF16), preferred_element_type=F32)
        o_ref[:, s, :] = _pack(a[:r], a[r:])


def _plane(idx, r, g, c):
    return BS((None, r, g, c), lambda i: (idx, 0, i, 0))


def fft_stage_in(x4, idx, mats):
    r = FFT_RADIX
    c = x4.shape[-1]
    g = FFT_GROUP
    return pl.pallas_call(
        _fft_in_kernel,
        out_shape=SDS((r, r, c), U32),
        grid=(r // g,),
        in_specs=[_plane(idx, r, g, c), BS((g, 2 * r, r), lambda i: (i, 0, 0))],
        out_specs=BS((r, g, c), lambda i: (0, i, 0)),
        compiler_params=_params(("parallel",)),
        name="fft_stage_in",
    )(x4, mats)


def _fft_mid_kernel(*refs, spectrum_only):
    if spectrum_only:
        a_ref, f_ref, sc_ref, o_ref = refs
    else:
        a_ref, h_ref, f_ref, ft_ref, o_ref = refs
    r = FFT_RADIX
    for s in range(a_ref.shape[0]):
        ar, ai = _unpack(a_ref[s])
        x = jnp.dot(f_ref[...], jnp.concatenate([ar, ai], axis=0).astype(BF16), preferred_element_type=F32)
        xr, xi = x[:r], x[r:]
        if spectrum_only:
            o_ref[s] = _pack(xr * sc_ref[...], xi * sc_ref[...])
        else:
            hr, hi = _unpack(h_ref[s])
            yr = xr * hr - xi * hi
            yi = xr * hi + xi * hr
            bmat = jnp.dot(ft_ref[...], jnp.concatenate([yr, yi], axis=0).astype(BF16),
                           preferred_element_type=F32)
            o_ref[s] = _pack(bmat[:r], bmat[r:])


def fft_stage_mid(a_pk, fmat, fmat_t, h_pk=None, scale=None):
    r = FFT_RADIX
    c = a_pk.shape[-1]
    kb = 4
    blk = BS((kb, r, c), lambda i: (i, 0, 0))
    mat = BS((2 * r, 2 * r), lambda i: (0, 0))
    spectrum_only = h_pk is None
    if spectrum_only:
        in_specs, args = [blk, mat, BS((1, c), lambda i: (0, 0))], (a_pk, fmat, scale)
    else:
        in_specs, args = [blk, blk, mat, mat], (a_pk, h_pk, fmat, fmat_t)
    return pl.pallas_call(
        functools.partial(_fft_mid_kernel, spectrum_only=spectrum_only),
        out_shape=SDS((r, r, c), U32),
        grid=(r // kb,),
        in_specs=in_specs,
        out_specs=blk,
        compiler_params=_params(("parallel",)),
        name="fft_stage_mid",
    )(*args)


def _fft_out_kernel(b_ref, m_ref, v_ref, g_ref, d_ref, o_ref):
    for s in range(FFT_GROUP):
        br, bi = _unpack(b_ref[:, s, :])
        y = jnp.dot(m_ref[s], jnp.concatenate([br, bi], axis=0).astype(BF16), preferred_element_type=F32)
        o_ref[:, s, :] = (g_ref[:, s, :] * (y + v_ref[:, s, :] * d_ref[...])).astype(o_ref.dtype)


def fft_stage_out(b_pk, mats_t, v4, v_idx, gate4, gate_idx, skip, out_dtype):
    r = FFT_RADIX
    c = b_pk.shape[-1]
    g = FFT_GROUP
    blk = BS((r, g, c), lambda i: (0, i, 0))
    return pl.pallas_call(
        _fft_out_kernel,
        out_shape=SDS((r, r, c), out_dtype),
        grid=(r // g,),
        in_specs=[blk, BS((g, r, 2 * r), lambda i: (i, 0, 0)), _plane(v_idx, r, g, c),
                  _plane(gate_idx, r, g, c), BS((1, c), lambda i: (0, 0))],
        out_specs=blk,
        compiler_params=_params(("parallel",)),
        name="fft_stage_out",
    )(b_pk, mats_t, v4, gate4, skip)


def _dft_tables(full_input):
    r = FFT_RADIX
    n = r * r
    t1 = jnp.arange(r, dtype=I32)[:, None, None]
    k2 = jnp.arange(r, dtype=I32)[None, :, None]
    nt2 = r if full_input else r // 2
    t2 = jnp.arange(nt2, dtype=I32)[None, None, :]
    ang = (2.0 * math.pi / n) * (((r * t2 + t1) * k2) % n).astype(F32)
    c, s = jnp.cos(ang), jnp.sin(ang)
    if full_input:
        return jnp.concatenate([c, -s], axis=1).astype(BF16)
    top = jnp.concatenate([c, s], axis=2)
    bot = jnp.concatenate([-s, c], axis=2)
    mats = jnp.concatenate([top, bot], axis=1)
    mats_t = jnp.swapaxes(mats, 1, 2) * (1.0 / n)
    return mats.astype(BF16), mats_t.astype(BF16)


def _dft_radix():
    r = FFT_RADIX
    a = jnp.arange(r, dtype=I32)
    ang = (2.0 * math.pi / r) * ((a[:, None] * a[None, :]) % r).astype(F32)
    c, s = jnp.cos(ang), jnp.sin(ang)
    f = jnp.concatenate([jnp.concatenate([c, s], 1), jnp.concatenate([-s, c], 1)], 0)
    return f.astype(BF16), f.T.astype(BF16)


def _dense_dft(n, n_in, full_input):
    k = jnp.arange(n, dtype=I32)[:, None]
    t = jnp.arange(n_in, dtype=I32)[None, :]
    ang = (2.0 * math.pi / n) * ((k * t) % n).astype(F32)
    c, s = jnp.cos(ang), jnp.sin(ang)
    if full_input:
        return jnp.concatenate([c, -s], 0).astype(BF16)
    fwd = jnp.concatenate([jnp.concatenate([c, s], 1), jnp.concatenate([-s, c], 1)], 0)
    return fwd.astype(BF16), (fwd.T * (1.0 / n)).astype(BF16)


def fft_tables():
    return _dft_tables(False) + (_dft_tables(True),) + _dft_radix()


def hyena_long(vx, filt, l1, skip, tables):
    _, b, seq, c = vx.shape
    assert b == 2
    r = FFT_RADIX
    inv_l1 = 1.0 / l1[:2]
    if 2 * seq == r * r:
        mats, mats_t, mats_f, fmat, fmat_t = tables
        vx4 = vx.reshape(3, r, r, c)
        filt4 = filt.reshape(2, r, r, c)
        spec = [fft_stage_mid(fft_stage_in(filt4, o, mats_f), fmat, fmat_t, scale=inv_l1[o:o + 1])
                for o in range(2)]
        z4 = fft_stage_out(fft_stage_mid(fft_stage_in(vx4, 0, mats), fmat, fmat_t, h_pk=spec[0]),
                           mats_t, vx4, 0, vx4, 1, skip[0:1], F32)[None]
        y3 = fft_stage_out(fft_stage_mid(fft_stage_in(z4, 0, mats), fmat, fmat_t, h_pk=spec[1]),
                           mats_t, z4, 0, vx4, 2, skip[1:2], BF16)
        return y3.reshape(b, seq, c)
    n = 2 * seq
    fwd, inv = _dense_dft(n, seq, False)
    fwd_f = _dense_dft(n, n, True)
    v, x1, x2 = (vx[i].reshape(b * seq, c) for i in range(3))

    def conv(u, o):
        hs = matmul([fwd_f], filt[o][None], 0, n_out=c) * inv_l1[o:o + 1]
        xs = matmul([fwd], u[None], 0, n_out=c)
        hr, hi, xr, xi = hs[:n], hs[n:], xs[:n], xs[n:]
        ys = jnp.concatenate([xr * hr - xi * hi, xr * hi + xi * hr], 0)
        return matmul([inv], ys[None], 0, n_out=c)

    z = x1 * (conv(v, 0) + v * skip[0:1])
    y = x2 * (conv(z, 1) + z * skip[1:2])
    return y.astype(BF16).reshape(b, seq, c)


def _dwconv_kernel(x_ref, p_ref, n_ref, w_ref, b_ref, g_ref, be_ref, o_ref, win_ref, y_ref, sh_ref):
    i = pl.program_id(1)
    nt = pl.num_programs(1)
    ts = x_ref.shape[1]
    d = x_ref.shape[2]
    win_ref[0:HALO, :] = jnp.where(i == 0, 0.0, p_ref[0])
    win_ref[HALO:HALO + ts, :] = x_ref[0]
    win_ref[HALO + ts:, :] = jnp.where(i == nt - 1, 0.0, n_ref[0])
    pad = (CONF_K - 1) // 2
    first = HALO - pad
    rows = DW_ROWS
    def lane_block(cb, carry):
        cs = pl.ds(pl.multiple_of(cb * LANES, LANES), LANES)
        wb = [jnp.broadcast_to(w_ref[j:j + 1, cs], (SUBLANES, LANES)) for j in range(CONF_K)]
        bb = jnp.broadcast_to(b_ref[:, cs], (SUBLANES, LANES))
        for r0 in range(0, ts, rows):
            acc = jnp.zeros((rows // SUBLANES, SUBLANES, LANES), F32)
            for r in range(SUBLANES):
                offs = [o for o in range(first, first + CONF_K) if o % SUBLANES == r]
                span = rows + (max(offs) // SUBLANES) * SUBLANES
                sh_ref[r, 0:span, :] = win_ref[pl.ds(r0 + r, span), cs]
                for o in offs:
                    a8 = (o // SUBLANES) * SUBLANES
                    tap = sh_ref[r, a8:a8 + rows, :].reshape(rows // SUBLANES, SUBLANES, LANES)
                    acc = acc + tap * wb[o - first]
            y_ref[r0:r0 + rows, cs] = (acc + bb).reshape(rows, LANES)
        return carry

    lax.fori_loop(0, d // LANES, lane_block, 0)
    y = y_ref[...]
    mu = jnp.mean(y, axis=-1, keepdims=True)
    yc = y - mu
    var = jnp.mean(yc * yc, axis=-1, keepdims=True)
    a = yc * lax.rsqrt(var + NORM_EPS) * g_ref[...] + be_ref[...]
    o_ref[0] = (a * jax.nn.sigmoid(a)).astype(o_ref.dtype)


def conformer_dwconv(x, w, b, g, be):
    bsz, s, d = x.shape
    ts = LN_ROWS
    hb = ts // HALO
    nhb = s // HALO
    vec = lambda r: BS((r, d), lambda bi, i: (0, 0))
    return pl.pallas_call(
        _dwconv_kernel,
        out_shape=SDS((bsz, s, d), BF16),
        grid=(bsz, s // ts),
        in_specs=[BS((1, ts, d), lambda bi, i: (bi, i, 0)),
                  BS((1, HALO, d), lambda bi, i: (bi, jnp.maximum(i * hb - 1, 0), 0)),
                  BS((1, HALO, d), lambda bi, i: (bi, jnp.minimum((i + 1) * hb, nhb - 1), 0)),
                  vec(CONF_K), vec(1), vec(1), vec(1)],
        out_specs=BS((1, ts, d), lambda bi, i: (bi, i, 0)),
        scratch_shapes=[pltpu.VMEM((ts + 2 * HALO, d), F32), pltpu.VMEM((ts, d), F32),
                        pltpu.VMEM((SUBLANES, DW_ROWS + 2 * HALO, LANES), F32)],
        compiler_params=_params(("parallel", "parallel")),
        name="conformer_dwconv",
    )(x, x, x, w, b.reshape(1, d), g.reshape(1, d), be.reshape(1, d))


def _layer_norm(v, g, b):
    mu = jnp.mean(v, axis=-1, keepdims=True)
    vc = v - mu
    var = jnp.mean(vc * vc, axis=-1, keepdims=True)
    return vc * lax.rsqrt(var + NORM_EPS) * g + b


def _route(tok, rw_ref, rb_ref):
    rows = tok.shape[0]
    logits_t = _dot3(tok, rw_ref[0], rw_ref[1]).T
    aff = jax.nn.sigmoid(logits_t[:N_EXPERTS])
    sel = aff + rb_ref[:N_EXPERTS]
    sc = [sel[e:e + 1] for e in range(N_EXPERTS)]
    ac = [aff[e:e + 1] for e in range(N_EXPERTS)]
    scores = []
    for gi in range(N_GROUPS):
        m = sc[gi * 4:(gi + 1) * 4]
        best = m[0] + m[1]
        for a in range(4):
            for bb in range(a + 1, 4):
                if (a, bb) != (0, 1):
                    best = jnp.maximum(best, m[a] + m[bb])
        scores.append(best)
    top = scores[0]
    grp = jnp.zeros((1, rows), I32)
    for gi in range(1, N_GROUPS):
        upd = scores[gi] > top
        grp = jnp.where(upd, gi, grp)
        top = jnp.where(upd, scores[gi], top)

    def pick(cols, k):
        out = cols[k]
        for gi in range(1, N_GROUPS):
            out = jnp.where(grp == gi, cols[gi * 4 + k], out)
        return out

    ms = [pick(sc, k) for k in range(4)]
    ma = [pick(ac, k) for k in range(4)]

    def argmax4(vals):
        bv, bi, ba = vals[0], jnp.zeros((1, rows), I32), ma[0]
        for k in range(1, 4):
            upd = vals[k] > bv
            bi = jnp.where(upd, k, bi)
            ba = jnp.where(upd, ma[k], ba)
            bv = jnp.where(upd, vals[k], bv)
        return bi, ba

    i1, a1 = argmax4(ms)
    i2, a2 = argmax4([jnp.where(i1 == k, -jnp.inf, ms[k]) for k in range(4)])
    tot = a1 + a2
    fields = [(grp * 4 + i1).astype(F32), (grp * 4 + i2).astype(F32), a1 / tot, a2 / tot]
    route_t = jnp.concatenate(fields + [jnp.zeros((LANES - len(fields), rows), F32)], axis=0)
    return route_t.T


def _ln_router_kernel(x_ref, y_ref, g_ref, lg_ref, lb_ref, sc_ref, sh_ref, rw_ref, rb_ref,
                      xo_ref, tok_ref, rt_ref):
    v = DEEPNORM_ALPHA * x_ref[0] + g_ref[0] * y_ref[0].astype(F32)
    xn = _layer_norm(v, lg_ref[...], lb_ref[...])
    xo_ref[0] = xn
    tok = xn * (1.0 + sc_ref[0]) + sh_ref[0]
    half = tok.shape[1] // 2
    tok_ref[0] = _pack(tok[:, :half], tok[:, half:])
    rt_ref[0] = _route(tok, rw_ref, rb_ref)


def ln_router(x, y, gate, ln_g, ln_b, sc, sh, rw_pad, rb_pad):
    b, s, d = x.shape
    ts = min(LN_ROWS, s)
    tile = BS((1, ts, d), lambda bi, i: (bi, i, 0))
    vec = BS((1, 1, d), lambda bi, i: (bi, 0, 0))
    par = BS((1, d), lambda bi, i: (0, 0))
    return pl.pallas_call(
        _ln_router_kernel,
        out_shape=[SDS((b, s, d), F32), SDS((b, s, d // 2), U32), SDS((b, s, LANES), F32)],
        grid=(b, s // ts),
        in_specs=[tile, tile, vec, par, par, vec, vec,
                  BS((2, d, LANES), lambda bi, i: (0, 0, 0)), BS((LANES, ts), lambda bi, i: (0, 0))],
        out_specs=[tile, BS((1, ts, d // 2), lambda bi, i: (bi, i, 0)), BS((1, ts, LANES), lambda bi, i: (bi, i, 0))],
        compiler_params=_params(("parallel", "parallel")),
        name="ln_router",
    )(x, y.reshape(b, s, d), gate, ln_g.reshape(1, d), ln_b.reshape(1, d), sc, sh, rw_pad, rb_pad)


def _row_copy_wait(src_rows_ref, dst_any, sem, n_rows):
    pltpu.make_async_copy(src_rows_ref, dst_any.at[pl.ds(0, n_rows), :], sem).wait()


def _dispatch_kernel(pos_ref, lt_ref, na_ref, lat_ref, ctx_ref, xs_hbm, zero_ref, sem, zsem, *, n_lat, n_tok):
    i = pl.program_id(0)
    ts = lat_ref.shape[0]
    n_tiles = xs_hbm.shape[0] // GMM_ROWS

    def zero_tile(t):
        start = pl.multiple_of(t * GMM_ROWS, GMM_ROWS)
        return pltpu.make_async_copy(zero_ref, xs_hbm.at[pl.ds(start, GMM_ROWS), :], zsem)

    @pl.when(i == 0)
    def _():
        zero_ref[...] = jnp.zeros_like(zero_ref)
        for e in range(N_EXPERTS):
            @pl.when(lt_ref[e] >= 0)
            def _():
                zero_tile(lt_ref[e]).start()

        def start_tail(t, c):
            zero_tile(t).start()
            return c

        lax.fori_loop(na_ref[0], n_tiles, start_tail, 0)
        for e in range(N_EXPERTS):
            @pl.when(lt_ref[e] >= 0)
            def _():
                zero_tile(0).wait()

        def wait_tail(t, c):
            zero_tile(0).wait()
            return c

        lax.fori_loop(na_ref[0], n_tiles, wait_tail, 0)

    def scatter(src_ref):
        base = i * ts

        def body(r, c):
            for k in range(2):
                dst = pos_ref[k * n_tok + base + r]
                pltpu.make_async_copy(src_ref.at[pl.ds(r, 1), :], xs_hbm.at[pl.ds(dst, 1), :], sem).start()
            return c

        lax.fori_loop(0, ts, body, 0, unroll=ROW_COPY_UNROLL)
        for _ in range(2):
            _row_copy_wait(src_ref, xs_hbm, sem, ts)

    if ctx_ref is None:
        scatter(lat_ref)
    else:
        @pl.when(i < n_lat)
        def _():
            scatter(lat_ref)

        @pl.when(i >= n_lat)
        def _():
            scatter(ctx_ref)


def dispatch(pos, last_tile, n_active, tok_lat, tok_ctx, n_rows):
    t_lat, d = tok_lat.shape
    ts = LN_ROWS
    n_lat = t_lat // ts
    n_ctx = 0 if tok_ctx is None else tok_ctx.shape[0] // ts
    n_tok = t_lat + (0 if tok_ctx is None else tok_ctx.shape[0])
    in_specs = [BS((ts, d), lambda i, p, l, na: (jnp.minimum(i, n_lat - 1), 0))]
    args = [tok_lat]
    if tok_ctx is not None:
        in_specs.append(BS((ts, d), lambda i, p, l, na: (jnp.maximum(i - n_lat, 0), 0)))
        args.append(tok_ctx)

    def kern(pos_ref, lt_ref, na_ref, *refs):
        if tok_ctx is None:
            lat_ref, xs_hbm, zero_ref, sem, zsem = refs
            ctx_ref = None
        else:
            lat_ref, ctx_ref, xs_hbm, zero_ref, sem, zsem = refs
        _dispatch_kernel(pos_ref, lt_ref, na_ref, lat_ref, ctx_ref, xs_hbm, zero_ref, sem, zsem,
                         n_lat=n_lat, n_tok=n_tok)

    return pl.pallas_call(
        kern,
        out_shape=SDS((n_rows, d), tok_lat.dtype),
        grid_spec=pltpu.PrefetchScalarGridSpec(
            num_scalar_prefetch=3, grid=(n_lat + n_ctx,),
            in_specs=in_specs,
            out_specs=BS(memory_space=pl.ANY),
            scratch_shapes=[pltpu.VMEM((GMM_ROWS, d), tok_lat.dtype), pltpu.SemaphoreType.DMA(()),
                            pltpu.SemaphoreType.DMA(())]),
        compiler_params=_params(("arbitrary",)),
        name="dispatch",
    )(pos, last_tile, n_active, *args)


def _gmm_kernel(te_ref, na_ref, x_ref, w1_ref, w3_ref, w2_ref, o_ref, acc_ref):
    i = pl.program_id(0)
    j = pl.program_id(1)
    dh = x_ref.shape[1]

    def half_mlp():
        xa, xb = _unpack(x_ref[...])
        xa = xa.astype(BF16)
        xb = xb.astype(BF16)

        def up(w_ref):
            return (jnp.dot(xa, w_ref[0, 0, :dh, :].astype(BF16), preferred_element_type=F32)
                    + jnp.dot(xb, w_ref[0, 0, dh:, :].astype(BF16), preferred_element_type=F32))

        a = up(w1_ref)
        b = up(w3_ref)
        h = (a * jax.nn.sigmoid(a) * b).astype(BF16)
        return jnp.dot(h, w2_ref[0, 0].astype(BF16), preferred_element_type=F32)

    @pl.when(jnp.logical_and(i < na_ref[0], j == 0))
    def _():
        acc_ref[...] = half_mlp()

    @pl.when(jnp.logical_and(i < na_ref[0], j != 0))
    def _():
        tot = acc_ref[...] + half_mlp()
        o_ref[...] = _pack(tot[:, :dh], tot[:, dh:])

    @pl.when(jnp.logical_and(i >= na_ref[0], j == 0))
    def _():
        o_ref[...] = jnp.zeros_like(o_ref)


def grouped_mlp(tile_expert, n_active, xs, w1, w3, w2, layer):
    n_rows, dh = xs.shape
    d = 2 * dh
    n_tiles = n_rows // GMM_ROWS

    def tile(i, te, na):
        return jnp.minimum(i, na[0] - 1)

    def half(i, j, na):
        last = na[0] - 1
        return jnp.where(i <= last, jnp.where(i % 2 == 0, j, 1 - j), jnp.where(last % 2 == 0, 1, 0))

    def expert(i, te, na):
        return te[tile(i, te, na)]

    return pl.pallas_call(
        _gmm_kernel,
        out_shape=SDS((n_rows, dh), U32),
        grid_spec=pltpu.PrefetchScalarGridSpec(
            num_scalar_prefetch=2, grid=(n_tiles, 2),
            in_specs=[BS((GMM_ROWS, dh), lambda i, j, te, na: (tile(i, te, na), 0)),
                      BS((1, 1, d, GMM_HALF), lambda i, j, te, na: (layer, expert(i, te, na), 0, half(i, j, na))),
                      BS((1, 1, d, GMM_HALF), lambda i, j, te, na: (layer, expert(i, te, na), 0, half(i, j, na))),
                      BS((1, 1, GMM_HALF, d), lambda i, j, te, na: (layer, expert(i, te, na), half(i, j, na), 0))],
            out_specs=BS((GMM_ROWS, dh), lambda i, j, te, na: (i, 0)),
            scratch_shapes=[pltpu.VMEM((GMM_ROWS, d), F32)]),
        compiler_params=_params(("arbitrary", "arbitrary")),
        name="grouped_mlp",
    )(tile_expert, n_active, xs, w1, w3, w2)


def _combine_ln_kernel(pos_ref, x_ref, rt_ref, g_ref, lg_ref, lb_ref, sc_ref, sh_ref, ys_hbm,
                       xo_ref, h_ref, buf_ref, sem, *, tok_off, n_tok, tiles_per_batch, emit_h):
    ts = x_ref.shape[1]
    step = pl.program_id(0) * tiles_per_batch + pl.program_id(1)
    n_steps = pl.num_programs(0) * tiles_per_batch
    slot = step % 2

    def gather(tile, dst_slot):
        base = tok_off + tile * ts

        def body(r, c):
            for k in range(2):
                src = pos_ref[k * n_tok + base + r]
                pltpu.make_async_copy(ys_hbm.at[pl.ds(src, 1), :], buf_ref.at[dst_slot, k, pl.ds(r, 1), :],
                                      sem.at[dst_slot]).start()
            return c

        lax.fori_loop(0, ts, body, 0, unroll=ROW_COPY_UNROLL)

    @pl.when(step == 0)
    def _():
        gather(step, slot)

    @pl.when(step + 1 < n_steps)
    def _():
        gather(step + 1, 1 - slot)

    for k in range(2):
        pltpu.make_async_copy(ys_hbm.at[pl.ds(0, ts), :], buf_ref.at[slot, k], sem.at[slot]).wait()
    rt = rt_ref[0]
    halves = []
    for part in range(2):
        y0 = _unpack(buf_ref[slot, 0])[part]
        y1 = _unpack(buf_ref[slot, 1])[part]
        halves.append(rt[:, 2:3] * y0 + rt[:, 3:4] * y1)
    moe = jnp.concatenate(halves, axis=1)
    xn = _layer_norm(DEEPNORM_ALPHA * x_ref[0] + g_ref[0] * moe, lg_ref[...], lb_ref[...])
    xo_ref[0] = xn
    if emit_h:
        h_ref[0] = (xn * (1.0 + sc_ref[0]) + sh_ref[0]).astype(h_ref.dtype)
    else:
        h_ref[0] = jnp.zeros(h_ref.shape[1:], h_ref.dtype)


def combine_ln(pos, x, route, gate, ln_g, ln_b, sc, sh, ys, *, tok_off, n_tok, emit_h=True):
    b, s, d = x.shape
    ts = min(LN_ROWS, s)
    tpb = s // ts
    tile = BS((1, ts, d), lambda bi, i, p: (bi, i, 0))
    vec = BS((1, 1, d), lambda bi, i, p: (bi, 0, 0))
    par = BS((1, d), lambda bi, i, p: (0, 0))
    h_rows = ts if emit_h else SUBLANES
    return pl.pallas_call(
        functools.partial(_combine_ln_kernel, tok_off=tok_off, n_tok=n_tok, tiles_per_batch=tpb, emit_h=emit_h),
        out_shape=[SDS((b, s, d), F32), SDS((b, tpb * h_rows, d), BF16)],
        grid_spec=pltpu.PrefetchScalarGridSpec(
            num_scalar_prefetch=1, grid=(b, tpb),
            in_specs=[tile, BS((1, ts, LANES), lambda bi, i, p: (bi, i, 0)), vec, par, par, vec, vec,
                      BS(memory_space=pl.ANY)],
            out_specs=[tile, BS((1, h_rows, d), lambda bi, i, p: (bi, i, 0))],
            scratch_shapes=[pltpu.VMEM((2, 2, ts, d // 2), U32), pltpu.SemaphoreType.DMA((2,))]),
        compiler_params=_params(("arbitrary", "arbitrary")),
        name="combine_ln",
    )(pos, x, route, gate, ln_g.reshape(1, d), ln_b.reshape(1, d), sc, sh, ys)


def _rope_tables(n_tokens):
    rows = n_tokens // GRID_W
    row = jnp.repeat(jnp.arange(rows, dtype=F32), GRID_W)
    col = jnp.tile(jnp.arange(GRID_W, dtype=F32), rows)
    half = HEAD_DIM // 2
    inv = ROPE_THETA ** (-jnp.arange(0, half, 2, dtype=F32) / half)
    ang = jnp.repeat(jnp.concatenate([row[:, None] * inv, col[:, None] * inv], -1), 2, axis=-1)
    cos, sin = jnp.cos(ang), jnp.sin(ang)
    even = (jnp.arange(HEAD_DIM) % 2 == 0)[None, :]
    return cos, jnp.where(even, -sin, 0.0), jnp.where(even, 0.0, sin)


def _even_mixer(h, hc, ctx_out, rope_tables, fft_tables, w_in_all, q_g, k_g, conv_w, conv_b,
                f_w1, f_b1, f_w2, f_b2, f_w3, f_freq, skip, w_out_all, li):
    b, s, d = h.shape
    lc = hc.shape[1]
    qg = q_g.reshape(1, HEAD_DIM)
    kg = k_g.reshape(1, HEAD_DIM)
    n_in = w_in_all.shape[2]

    proj = matmul([h.reshape(b * s, d)], w_in_all, li, n_out=n_in, tn=768).reshape(b, s, n_in)
    proj_c = matmul([hc.reshape(b * lc, d)], w_in_all, li, n_out=n_in, tn=768).reshape(b, lc, n_in)
    q, k, v = qkv_prepare(proj, qg, kg, rope_tables, with_q=True)
    qc, kc, vc = qkv_prepare(proj_c, qg, kg, None, with_q=ctx_out)
    att = attention(q, jnp.concatenate([k, kc], axis=1), jnp.concatenate([v, vc], axis=1))

    def hyena(p, seq):
        vx = hyena_short_conv(p, conv_w, conv_b)
        filt, l1 = hyena_filter(seq, f_w1, f_b1, f_w2, f_b2, f_w3, f_freq)
        return hyena_long(vx, filt, l1, skip, fft_tables)

    hy = hyena(proj, s)
    y = matmul([att.reshape(b * s, ATTN_WIDTH), hy.reshape(b * s, HYENA_WIDTH)], w_out_all, li,
               n_out=d).reshape(b, s, d)
    if not ctx_out:
        return y, None
    att_c = attention(qc, kc, vc)
    hy_c = hyena(proj_c, lc)
    yc = matmul([att_c.reshape(b * lc, ATTN_WIDTH), hy_c.reshape(b * lc, HYENA_WIDTH)], w_out_all, li,
                n_out=d).reshape(b, lc, d)
    return y, yc


def _conformer(h, in_w, in_b, dw_w, dw_b, n_g, n_b, out_w, out_b, li):
    b, s, d = h.shape
    a = matmul([h.reshape(b * s, d)], in_w, li, n_out=d, bias=in_b, glu=True).reshape(b, s, d)
    a = conformer_dwconv(a, dw_w[li], dw_b[li], n_g[li], n_b[li])
    return matmul([a.reshape(b * s, d)], out_w, li, n_out=d, bias=out_b).reshape(b, s, d)


def _dispatch_plan(experts, n_tiles):
    n_tok = experts.shape[0]
    flat = experts.T.reshape(-1)
    hit = flat[:, None] == jnp.arange(N_EXPERTS, dtype=I32)[None, :]
    blk = LANES
    oh = hit.astype(F32).reshape(-1, blk, N_EXPERTS)
    strict_lower = (jnp.arange(blk)[:, None] > jnp.arange(blk)[None, :]).astype(F32)
    within = jnp.einsum("ij,bje->bie", strict_lower, oh, precision=HIGHEST)
    blk_tot = jnp.sum(oh, axis=1)
    blk_off = jnp.cumsum(blk_tot, axis=0) - blk_tot
    before = (within + blk_off[:, None, :]).reshape(-1, N_EXPERTS)
    rank = jnp.sum(jnp.where(hit, before, 0.0), axis=1).astype(I32)
    counts = jnp.sum(blk_tot, axis=0).astype(I32)
    padded = ((counts + GMM_ROWS - 1) // GMM_ROWS) * GMM_ROWS
    ends = jnp.cumsum(padded)
    starts = ends - padded
    pos = (starts[flat] + rank).astype(I32)
    n_active = (ends[-1] // GMM_ROWS).astype(I32).reshape(1)
    tile_start = jnp.arange(n_tiles, dtype=I32) * GMM_ROWS
    tile_expert = jnp.minimum(jnp.sum((ends[None, :] <= tile_start[:, None]).astype(I32), axis=1),
                              N_EXPERTS - 1).astype(I32)
    last_tile = jnp.where(padded > 0, ends // GMM_ROWS - 1, -1).astype(I32)
    del n_tok
    return pos, n_active, tile_expert, last_tile


def kernel(x, c, ctx, c_ctx, mod_w, mod_b, ln_g, ln_b, mix_in_w, q_norm_g, k_norm_g, hy_conv_w, hy_conv_b,
           hy_w1, hy_b1, hy_w2, hy_b2, hy_w3, hy_freq, hy_skip, mix_out_w, cf_in_w, cf_in_b, cf_dw_w, cf_dw_b,
           cf_ln_g, cf_ln_b, cf_out_w, cf_out_b, router_w, router_b, exp_w1, exp_w3, exp_w2):
    b, s, d = x.shape
    lc = ctx.shape[1]
    rope_tables = _rope_tables(s)
    dft = fft_tables()
    cvec =jnp.concatenate([c, c_ctx[None], jnp.zeros((SUBLANES - b - 1, d), F32)], 0)
    mods = mod_vectors(cvec, mod_w, mod_b)
    rw_pad = jnp.stack(_split(jnp.concatenate([router_w, jnp.zeros((d, LANES - N_EXPERTS), F32)], 1)))
    rb_pad = jnp.broadcast_to(jnp.concatenate([router_b, jnp.zeros((LANES - N_EXPERTS,), F32)])[:, None],
                              (LANES, LN_ROWS))

    def mod_of(l, j, is_ctx):
        m = mods[l, :, j * d:(j + 1) * d]
        rows = jnp.broadcast_to(m[b:b + 1], (b, d)) if is_ctx else m[:b]
        return rows.reshape(b, 1, d)

    h = modulate(x, mod_of(0, 1, False), mod_of(0, 0, False))
    hc = modulate(ctx, mod_of(0, 1, True), mod_of(0, 0, True))
    for l in range(DEPTH):
        ctx_needed = any(j > l and j % 2 == 0 for j in range(DEPTH))
        li = l // 2
        if l % 2 == 0:
            y, yc = _even_mixer(h, hc, ctx_needed, rope_tables, dft, mix_in_w, q_norm_g[li], k_norm_g[li],
                                hy_conv_w[li], hy_conv_b[li], hy_w1[li], hy_b1[li], hy_w2[li], hy_b2[li],
                                hy_w3[li], hy_freq[li], hy_skip[li], mix_out_w, li)
        else:
            cf = (cf_in_w, cf_in_b, cf_dw_w, cf_dw_b, cf_ln_g, cf_ln_b, cf_out_w, cf_out_b)
            y = _conformer(h, *cf, li)
            yc = _conformer(hc, *cf, li) if ctx_needed else None

        x, tok, route = ln_router(x, y, mod_of(l, 2, False), ln_g[l, 0], ln_b[l, 0],
                                  mod_of(l, 4, False), mod_of(l, 3, False), rw_pad, rb_pad)
        routes = [route.reshape(b * s, LANES)]
        tok_c = None
        if ctx_needed:
            ctx, tok_c, route_c = ln_router(ctx, yc, mod_of(l, 2, True), ln_g[l, 0], ln_b[l, 0],
                                            mod_of(l, 4, True), mod_of(l, 3, True), rw_pad, rb_pad)
            routes.append(route_c.reshape(b * lc, LANES))
            tok_c = tok_c.reshape(b * lc, d // 2)
        n_tok = b * s + (b * lc if ctx_needed else 0)
        n_tiles = (2 * n_tok) // GMM_ROWS + N_EXPERTS
        experts = jnp.concatenate([r[:, :2] for r in routes], 0).astype(I32)
        pos, n_active, tile_expert, last_tile = _dispatch_plan(experts, n_tiles)
        xs = dispatch(pos, last_tile, n_active, tok.reshape(b * s, d // 2), tok_c, n_tiles * GMM_ROWS)
        ys = grouped_mlp(tile_expert, n_active, xs, exp_w1, exp_w3, exp_w2, l)

        last = l == DEPTH - 1
        nl = min(l + 1, DEPTH - 1)
        x, h = combine_ln(pos, x, route, mod_of(l, 5, False), ln_g[l, 1], ln_b[l, 1],
                          mod_of(nl, 1, False), mod_of(nl, 0, False), ys, tok_off=0, n_tok=n_tok,
                          emit_h=not last)
        if ctx_needed:
            ctx, hc = combine_ln(pos, ctx, route_c, mod_of(l, 5, True), ln_g[l, 1], ln_b[l, 1],
                                 mod_of(nl, 1, True), mod_of(nl, 0, True), ys, tok_off=b * s, n_tok=n_tok)
    return x
```

```python
import functools
import math

import jax
import jax.numpy as jnp
from jax import lax
from jax.experimental import pallas as pl
from jax.experimental.pallas import tpu as pltpu

F32 = jnp.float32
BF16 = jnp.bfloat16
U32 = jnp.uint32
I32 = jnp.int32
HIGHEST = lax.Precision.HIGHEST
SDS = jax.ShapeDtypeStruct
BS = pl.BlockSpec

D_MODEL = 2048
DEPTH = 4
GRID_W = 64
HEAD_DIM = 128
N_Q_HEADS = 8
N_KV_HEADS = 2
Q_PER_KV = N_Q_HEADS // N_KV_HEADS
ATTN_WIDTH = N_Q_HEADS * HEAD_DIM
KV_WIDTH = N_KV_HEADS * HEAD_DIM
ATTN_SCALE = HEAD_DIM ** -0.5
ROPE_THETA = 10000.0
HYENA_WIDTH = D_MODEL - ATTN_WIDTH
HYENA_EMB_DIM = 33
HYENA_FILTER_HIDDEN = 64
HYENA_FAST_DECAY_PCT = 0.3
HYENA_SLOW_DECAY_PCT = 1.5
HYENA_TARGET = 1e-2
U_OFFSET = ATTN_WIDTH + 2 * KV_WIDTH
CONF_K = 31
N_EXPERTS = 16
N_GROUPS = 4
EXPERTS_PER_GROUP = 4
D_EXPERT = 1024
DEEPNORM_ALPHA = (2 * DEPTH) ** 0.25
N_MOD = 6
NORM_EPS = 1e-6
LOG2_E = math.log2(math.e)

LANES = 128
SUBLANES = 8
VMEM_LIMIT_BYTES = 56 * 1024 * 1024

LN_ROWS = 256
MM_ROWS = 1024
GMM_ROWS = 512
GMM_HALF = D_EXPERT // 2
FFT_RADIX = 128
FFT_GROUP = 8
ATT_Q_ROWS = 512
ATT_K_ROWS = 768
VT_ROWS = HEAD_DIM + 16
HALO = 16
DW_ROWS = 64
ROW_COPY_UNROLL = 8


def _params(semantics):
    return pltpu.CompilerParams(dimension_semantics=semantics, vmem_limit_bytes=VMEM_LIMIT_BYTES)


def _split(x):
    hi = x.astype(BF16)
    return hi, (x - hi.astype(F32)).astype(BF16)


def _dot3(x, w_hi, w_lo):
    x_hi, x_lo = _split(x)
    dot = functools.partial(jnp.dot, preferred_element_type=F32)
    return dot(x_hi, w_hi) + (dot(x_hi, w_lo) + dot(x_lo, w_hi))


def _mod_kernel(c_ref, w_ref, b_ref, o_ref):
    c = c_ref[...]
    s = c * jax.nn.sigmoid(c)
    o_ref[0] = jnp.dot(s, w_ref[0], precision=HIGHEST, preferred_element_type=F32) + b_ref[0]


def mod_vectors(cvec, mod_w, mod_b):
    depth, d, n = mod_w.shape
    tn = 1024
    return pl.pallas_call(
        _mod_kernel,
        out_shape=SDS((depth, SUBLANES, n), F32),
        grid=(depth, n // tn),
        in_specs=[BS((SUBLANES, d), lambda l, j: (0, 0)),
                  BS((1, d, tn), lambda l, j: (l, 0, j)),
                  BS((1, 1, tn), lambda l, j: (l, 0, j))],
        out_specs=BS((1, SUBLANES, tn), lambda l, j: (l, 0, j)),
        compiler_params=_params(("parallel", "parallel")),
        name="mod_vectors",
    )(cvec, mod_w, mod_b.reshape(depth, 1, n))


def _modulate_kernel(x_ref, sc_ref, sh_ref, o_ref):
    o_ref[0] = (x_ref[0] * (1.0 + sc_ref[0]) + sh_ref[0]).astype(o_ref.dtype)


def modulate(x, sc, sh):
    b, s, d = x.shape
    ts = min(LN_ROWS, s)
    vec = BS((1, 1, d), lambda bi, i: (bi, 0, 0))
    return pl.pallas_call(
        _modulate_kernel,
        out_shape=SDS((b, s, d), BF16),
        grid=(b, s // ts),
        in_specs=[BS((1, ts, d), lambda bi, i: (bi, i, 0)), vec, vec],
        out_specs=BS((1, ts, d), lambda bi, i: (bi, i, 0)),
        compiler_params=_params(("parallel", "parallel")),
        name="modulate",
    )(x, sc, sh)


def _mm_kernel(*refs, k_sizes, has_bias, glu):
    n_a = len(k_sizes)
    a_refs = refs[:n_a]
    pos = n_a
    w_refs = refs[pos:pos + (2 if glu else 1)]
    pos += len(w_refs)
    b_refs = ()
    if has_bias:
        b_refs = refs[pos:pos + len(w_refs)]
        pos += len(w_refs)
    o_ref = refs[pos]
    wbf_refs = refs[pos + 1:]

    @pl.when(pl.program_id(1) == 0)
    def _():
        for w_ref, wbf in zip(w_refs, wbf_refs):
            wbf[...] = w_ref[0].astype(BF16)

    def linear(which):
        acc = None
        off = 0
        for a_ref, k in zip(a_refs, k_sizes):
            part = jnp.dot(a_ref[...].astype(BF16), wbf_refs[which][off:off + k, :],
                           preferred_element_type=F32)
            acc = part if acc is None else acc + part
            off += k
        if has_bias:
            acc = acc + b_refs[which][0]
        return acc

    out = linear(0)
    if glu:
        out = out * jax.nn.sigmoid(linear(1))
    o_ref[...] = out.astype(o_ref.dtype)


def matmul(a_list, w, layer, *, n_out, col_off=0, bias=None, glu=False, out_dtype=F32, tn=512):
    m = a_list[0].shape[0]
    k_sizes = tuple(a.shape[1] for a in a_list)
    k_total = sum(k_sizes)
    assert w.shape[1] == k_total and n_out % tn == 0 and col_off % tn == 0
    tm = min(MM_ROWS, m)
    assert m % tm == 0
    nb = n_out // tn
    cb = col_off // tn
    in_specs = [BS((tm, k), lambda j, i: (i, 0)) for k in k_sizes]
    args = list(a_list)
    n_w = 2 if glu else 1
    for h in range(n_w):
        in_specs.append(BS((1, k_total, tn), lambda j, i, h=h: (layer, 0, cb + h * nb + j)))
        args.append(w)
    if bias is not None:
        bias3 = bias.reshape(bias.shape[0], 1, bias.shape[1])
        for h in range(n_w):
            in_specs.append(BS((1, 1, tn), lambda j, i, h=h: (layer, 0, cb + h * nb + j)))
            args.append(bias3)
    return pl.pallas_call(
        functools.partial(_mm_kernel, k_sizes=k_sizes, has_bias=bias is not None, glu=glu),
        out_shape=SDS((m, n_out), out_dtype),
        grid=(nb, m // tm),
        in_specs=in_specs,
        out_specs=BS((tm, tn), lambda j, i: (i, j)),
        scratch_shapes=[pltpu.VMEM((k_total, tn), BF16) for _ in range(n_w)],
        compiler_params=_params(("parallel", "arbitrary")),
        name="matmul_glu" if glu else "matmul",
    )(*args)


def _qkv_kernel(*refs, rope, n_q_heads, n_k_heads):
    it = iter(refs)
    q_ref = next(it) if n_q_heads else None
    kv_ref = next(it)
    qg_ref = next(it)
    kg_ref = next(it)
    cos_ref = sin_next_ref = sin_prev_ref = None
    if rope:
        cos_ref = next(it)
        sin_next_ref = next(it)
        sin_prev_ref = next(it)
    qo_ref = next(it) if n_q_heads else None
    ko_ref = next(it)
    vo_ref = next(it)

    def prep(xh, g, scale):
        ms = jnp.mean(xh * xh, axis=-1, keepdims=True)
        xn = xh * lax.rsqrt(ms + NORM_EPS) * g
        if rope:
            xn = (xn * cos_ref[...] + pltpu.roll(xn, HEAD_DIM - 1, 1) * sin_next_ref[...]
                  + pltpu.roll(xn, 1, 1) * sin_prev_ref[...])
        if scale != 1.0:
            xn = xn * scale
        return xn.astype(BF16)

    for h in range(n_q_heads):
        sl = slice(h * HEAD_DIM, (h + 1) * HEAD_DIM)
        qo_ref[0, :, sl] = prep(q_ref[0, :, sl], qg_ref[...], ATTN_SCALE * LOG2_E)
    for h in range(n_k_heads):
        sl = slice(h * HEAD_DIM, (h + 1) * HEAD_DIM)
        ko_ref[0, :, sl] = prep(kv_ref[0, :, sl], kg_ref[...], 1.0)
    vo_ref[0] = kv_ref[0, :, KV_WIDTH:].astype(BF16)


def qkv_prepare(proj, q_gain, k_gain, rope_tables, *, with_q):
    rope = rope_tables is not None
    b, s, _ = proj.shape
    ts = min(512, s)
    n_q = N_Q_HEADS if with_q else 0
    in_specs, args = [], []
    if with_q:
        in_specs.append(BS((1, ts, ATTN_WIDTH), lambda bi, i: (bi, i, 0)))
        args.append(proj)
    in_specs.append(BS((1, ts, 2 * KV_WIDTH), lambda bi, i: (bi, i, ATTN_WIDTH // (2 * KV_WIDTH))))
    args.append(proj)
    gain = BS((1, HEAD_DIM), lambda bi, i: (0, 0))
    in_specs += [gain, gain]
    args += [q_gain, k_gain]
    if rope:
        tab = BS((ts, HEAD_DIM), lambda bi, i: (i, 0))
        in_specs += [tab] * len(rope_tables)
        args += list(rope_tables)
    out_shape, out_specs = [], []
    if with_q:
        out_shape.append(SDS((b, s, ATTN_WIDTH), BF16))
        out_specs.append(BS((1, ts, ATTN_WIDTH), lambda bi, i: (bi, i, 0)))
    for _ in range(2):
        out_shape.append(SDS((b, s, KV_WIDTH), BF16))
        out_specs.append(BS((1, ts, KV_WIDTH), lambda bi, i: (bi, i, 0)))
    res = pl.pallas_call(
        functools.partial(_qkv_kernel, rope=rope, n_q_heads=n_q, n_k_heads=N_KV_HEADS),
        out_shape=out_shape,
        grid=(b, s // ts),
        in_specs=in_specs,
        out_specs=out_specs,
        compiler_params=_params(("parallel", "parallel")),
        name="qkv_prepare",
    )(*args)
    return res if with_q else (None,) + tuple(res)


def _attn_kernel(q_ref, k_ref, vt_ref, o_ref, s_ref, *, tk):
    tq = q_ref.shape[1]
    skv = k_ref.shape[1]
    cols = Q_PER_KV * tq
    n = skv // tk
    qs = jnp.concatenate([q_ref[0, :, h * HEAD_DIM:(h + 1) * HEAD_DIM] for h in range(Q_PER_KV)], axis=0)

    def scores(c, slot):
        start = pl.multiple_of(c * tk, tk)
        s_ref[slot] = lax.dot_general(k_ref[0, pl.ds(start, tk), :], qs, (((1,), (1,)), ((), ())),
                                      preferred_element_type=F32)

    def consume(c, slot, m, acc):
        start = pl.multiple_of(c * tk, tk)
        vt = vt_ref[0, 0, :, pl.ds(start, tk)]
        m_new = jnp.maximum(m, jnp.max(s_ref[slot], axis=0, keepdims=True))
        p = jnp.exp2(s_ref[slot] - m_new).astype(BF16)
        alpha = jnp.exp2(m - m_new)
        return m_new, alpha * acc + jnp.dot(vt, p, preferred_element_type=F32)

    def pair(i, carry):
        m, acc = carry
        c = 2 * i
        scores(c + 1, 1)
        m, acc = consume(c, 0, m, acc)
        scores(c + 2, 0)
        return consume(c + 1, 1, m, acc)

    scores(0, 0)
    init = (jnp.full((1, cols), -jnp.inf, F32), jnp.zeros((VT_ROWS, cols), F32))
    m, acc = lax.fori_loop(0, n // 2, pair, init)
    _, acc = consume(n - 1, 0, m, acc)
    out = (acc[:HEAD_DIM] / acc[HEAD_DIM:HEAD_DIM + 1]).T
    for h in range(Q_PER_KV):
        o_ref[0, :, h * HEAD_DIM:(h + 1) * HEAD_DIM] = out[h * tq:(h + 1) * tq].astype(o_ref.dtype)


def attention(q, k, v):
    b, s, _ = q.shape
    skv = k.shape[1]
    tq = min(ATT_Q_ROWS, s)
    tk = ATT_K_ROWS if skv % ATT_K_ROWS == 0 else skv
    n = skv // tk
    assert skv % tk == 0 and s % tq == 0 and n % 2 == 1
    gw = Q_PER_KV * HEAD_DIM
    vt = jnp.swapaxes(v, 1, 2).reshape(b, N_KV_HEADS, HEAD_DIM, skv)
    vt = jnp.concatenate([vt, jnp.ones((b, N_KV_HEADS, VT_ROWS - HEAD_DIM, skv), BF16)], axis=2)
    return pl.pallas_call(
        functools.partial(_attn_kernel, tk=tk),
        out_shape=SDS((b, s, ATTN_WIDTH), BF16),
        grid=(b, N_KV_HEADS, s // tq),
        in_specs=[BS((1, tq, gw), lambda bi, h, i: (bi, i, h)),
                  BS((1, skv, HEAD_DIM), lambda bi, h, i: (bi, 0, h)),
                  BS((1, 1, VT_ROWS, skv), lambda bi, h, i: (bi, h, 0, 0))],
        out_specs=BS((1, tq, gw), lambda bi, h, i: (bi, i, h)),
        scratch_shapes=[pltpu.VMEM((2, tk, Q_PER_KV * tq), F32)],
        compiler_params=_params(("parallel", "parallel", "parallel")),
        name="attention",
    )(q, k, vt)


def _conv3_kernel(u_ref, w_ref, b_ref, o_ref):
    x = u_ref[0]
    s = x.shape[0]
    row = lax.broadcasted_iota(I32, x.shape, 0)
    prev = jnp.where(row == 0, 0.0, pltpu.roll(x, 1, 0))
    nxt = jnp.where(row == s - 1, 0.0, pltpu.roll(x, s - 1, 0))
    o_ref[0, 0] = prev * w_ref[0:1, :] + x * w_ref[1:2, :] + nxt * w_ref[2:3, :] + b_ref[...]


def hyena_short_conv(proj, conv_w, conv_b):
    b, s, _ = proj.shape
    cblocks = HYENA_WIDTH // LANES
    return pl.pallas_call(
        _conv3_kernel,
        out_shape=SDS((3, b, s, HYENA_WIDTH), F32),
        grid=(b, 3 * cblocks),
        in_specs=[BS((1, s, LANES), lambda bi, c: (bi, 0, U_OFFSET // LANES + c)),
                  BS((3, LANES), lambda bi, c: (0, c)),
                  BS((1, LANES), lambda bi, c: (0, c))],
        out_specs=BS((1, 1, s, LANES), lambda bi, c: (c // cblocks, bi, 0, c % cblocks)),
        compiler_params=_params(("parallel", "parallel")),
        name="hyena_short_conv",
    )(proj, conv_w, conv_b.reshape(1, -1))


def _filter_kernel(z_ref, w1_ref, b1_ref, w2_ref, b2_ref, w3_ref, fr_ref, dl_ref, k_ref, l1_ref, *, seq_len):
    i = pl.program_id(0)
    rows = z_ref.shape[0]
    z = z_ref[...]
    fr = fr_ref[...]
    a = jnp.sin(fr * (_dot3(z, w1_ref[0], w1_ref[1]) + b1_ref[...]))
    a = jnp.sin(fr * (_dot3(a, w2_ref[0], w2_ref[1]) + b2_ref[...]))
    h = _dot3(a, w3_ref[0], w3_ref[1])
    decay = jnp.exp(-z[:, 0:1] * dl_ref[...])
    n = i * rows + lax.broadcasted_iota(I32, (rows, 1), 0)
    keep = jnp.where(n == seq_len, 0.0, 1.0)

    @pl.when(i == 0)
    def _():
        l1_ref[...] = jnp.zeros_like(l1_ref)

    for o in range(2):
        ko = h[:, o * HYENA_WIDTH:(o + 1) * HYENA_WIDTH] * decay * keep
        k_ref[o] = ko
        l1_ref[o:o + 1, :] += jnp.sum(jnp.abs(ko), axis=0, keepdims=True)


def hyena_filter(seq_len, f_w1, f_b1, f_w2, f_b2, f_w3, f_freq):
    n2 = 2 * seq_len
    n = jnp.arange(n2, dtype=I32)
    lag = jnp.where(n < seq_len, n, n2 - n).astype(F32)
    t01 = lag / max(seq_len - 1, 1)
    bands = (HYENA_EMB_DIM - 1) // 2
    f = jnp.linspace(1e-4, bands - 1, bands, dtype=F32)
    fw = (2.0 * math.pi * lag / seq_len)[:, None] * f[None, :]
    z = jnp.concatenate([t01[:, None], jnp.cos(fw), -jnp.sin(fw),
                         jnp.zeros((n2, LANES - HYENA_EMB_DIM), F32)], -1)
    w1p = jnp.concatenate([f_w1, jnp.zeros((LANES - HYENA_EMB_DIM, HYENA_FILTER_HIDDEN), F32)], 0)
    max_decay = math.log(HYENA_TARGET) / HYENA_FAST_DECAY_PCT
    min_decay = math.log(HYENA_TARGET) / HYENA_SLOW_DECAY_PCT
    deltas = jnp.abs(jnp.linspace(min_decay, max_decay, HYENA_WIDTH, dtype=F32)).reshape(1, -1)
    rows = min(512, seq_len)
    half = seq_len // rows
    hid = HYENA_FILTER_HIDDEN
    full = lambda shp: BS(shp, lambda i: (0,) * len(shp))
    return pl.pallas_call(
        functools.partial(_filter_kernel, seq_len=seq_len),
        out_shape=[SDS((2, n2, HYENA_WIDTH), F32), SDS((SUBLANES, HYENA_WIDTH), F32)],
        grid=(n2 // rows,),
        in_specs=[BS((rows, LANES), lambda i: (i, 0)), full((2, LANES, hid)), full((1, hid)), full((2, hid, hid)),
                  full((1, hid)), BS((2, hid, 2 * HYENA_WIDTH), lambda i: (0, 0, i // half)), full((1, hid)),
                  full((1, HYENA_WIDTH))],
        out_specs=[BS((2, rows, HYENA_WIDTH), lambda i: (0, i, 0)), full((SUBLANES, HYENA_WIDTH))],
        compiler_params=_params(("arbitrary",)),
        name="hyena_filter",
    )(z, jnp.stack(_split(w1p)), f_b1.reshape(1, -1), jnp.stack(_split(f_w2)), f_b2.reshape(1, -1),
      jnp.stack(_split(f_w3)), f_freq.reshape(1, -1), deltas)


def _pack(re, im):
    hi = lax.bitcast_convert_type(re.astype(BF16).astype(F32), U32)
    lo = lax.bitcast_convert_type(im.astype(BF16).astype(F32), U32)
    return hi | (lo >> 16)


def _unpack(word):
    re = lax.bitcast_convert_type(word & jnp.uint32(0xFFFF0000), F32)
    im = lax.bitcast_convert_type(word << 16, F32)
    return re, im


def _fft_in_kernel(x_ref, m_ref, o_ref):
    r = FFT_RADIX
    for s in range(FFT_GROUP):
        a = jnp.dot(m_ref[s], x_ref[:, s, :].astype(BF16), preferred_element_type=F32)
        o_ref[:, s, :] = _pack(a[:r], a[r:])


def _plane(idx, r, g, c):
    return BS((None, r, g, c), lambda i: (idx, 0, i, 0))


def fft_stage_in(x4, idx, mats):
    r = FFT_RADIX
    c = x4.shape[-1]
    g = FFT_GROUP
    return pl.pallas_call(
        _fft_in_kernel,
        out_shape=SDS((r, r, c), U32),
        grid=(r // g,),
        in_specs=[_plane(idx, r, g, c), BS((g, 2 * r, r), lambda i: (i, 0, 0))],
        out_specs=BS((r, g, c), lambda i: (0, i, 0)),
        compiler_params=_params(("parallel",)),
        name="fft_stage_in",
    )(x4, mats)


def _fft_mid_kernel(*refs, spectrum_only):
    if spectrum_only:
        a_ref, f_ref, sc_ref, o_ref = refs
    else:
        a_ref, h_ref, f_ref, ft_ref, o_ref = refs
    r = FFT_RADIX
    for s in range(a_ref.shape[0]):
        ar, ai = _unpack(a_ref[s])
        x = jnp.dot(f_ref[...], jnp.concatenate([ar, ai], axis=0).astype(BF16), preferred_element_type=F32)
        xr, xi = x[:r], x[r:]
        if spectrum_only:
            o_ref[s] = _pack(xr * sc_ref[...], xi * sc_ref[...])
        else:
            hr, hi = _unpack(h_ref[s])
            yr = xr * hr - xi * hi
            yi = xr * hi + xi * hr
            bmat = jnp.dot(ft_ref[...], jnp.concatenate([yr, yi], axis=0).astype(BF16),
                           preferred_element_type=F32)
            o_ref[s] = _pack(bmat[:r], bmat[r:])


def fft_stage_mid(a_pk, fmat, fmat_t, h_pk=None, scale=None):
    r = FFT_RADIX
    c = a_pk.shape[-1]
    kb = 4
    blk = BS((kb, r, c), lambda i: (i, 0, 0))
    mat = BS((2 * r, 2 * r), lambda i: (0, 0))
    spectrum_only = h_pk is None
    if spectrum_only:
        in_specs, args = [blk, mat, BS((1, c), lambda i: (0, 0))], (a_pk, fmat, scale)
    else:
        in_specs, args = [blk, blk, mat, mat], (a_pk, h_pk, fmat, fmat_t)
    return pl.pallas_call(
        functools.partial(_fft_mid_kernel, spectrum_only=spectrum_only),
        out_shape=SDS((r, r, c), U32),
        grid=(r // kb,),
        in_specs=in_specs,
        out_specs=blk,
        compiler_params=_params(("parallel",)),
        name="fft_stage_mid",
    )(*args)


def _fft_out_kernel(b_ref, m_ref, v_ref, g_ref, d_ref, o_ref):
    for s in range(FFT_GROUP):
        br, bi = _unpack(b_ref[:, s, :])
        y = jnp.dot(m_ref[s], jnp.concatenate([br, bi], axis=0).astype(BF16), preferred_element_type=F32)
        o_ref[:, s, :] = (g_ref[:, s, :] * (y + v_ref[:, s, :] * d_ref[...])).astype(o_ref.dtype)


def fft_stage_out(b_pk, mats_t, v4, v_idx, gate4, gate_idx, skip, out_dtype):
    r = FFT_RADIX
    c = b_pk.shape[-1]
    g = FFT_GROUP
    blk = BS((r, g, c), lambda i: (0, i, 0))
    return pl.pallas_call(
        _fft_out_kernel,
        out_shape=SDS((r, r, c), out_dtype),
        grid=(r // g,),
        in_specs=[blk, BS((g, r, 2 * r), lambda i: (i, 0, 0)), _plane(v_idx, r, g, c),
                  _plane(gate_idx, r, g, c), BS((1, c), lambda i: (0, 0))],
        out_specs=blk,
        compiler_params=_params(("parallel",)),
        name="fft_stage_out",
    )(b_pk, mats_t, v4, gate4, skip)


def _dft_tables(full_input):
    r = FFT_RADIX
    n = r * r
    t1 = jnp.arange(r, dtype=I32)[:, None, None]
    k2 = jnp.arange(r, dtype=I32)[None, :, None]
    nt2 = r if full_input else r // 2
    t2 = jnp.arange(nt2, dtype=I32)[None, None, :]
    ang = (2.0 * math.pi / n) * (((r * t2 + t1) * k2) % n).astype(F32)
    c, s = jnp.cos(ang), jnp.sin(ang)
    if full_input:
        return jnp.concatenate([c, -s], axis=1).astype(BF16)
    top = jnp.concatenate([c, s], axis=2)
    bot = jnp.concatenate([-s, c], axis=2)
    mats = jnp.concatenate([top, bot], axis=1)
    mats_t = jnp.swapaxes(mats, 1, 2) * (1.0 / n)
    return mats.astype(BF16), mats_t.astype(BF16)


def _dft_radix():
    r = FFT_RADIX
    a = jnp.arange(r, dtype=I32)
    ang = (2.0 * math.pi / r) * ((a[:, None] * a[None, :]) % r).astype(F32)
    c, s = jnp.cos(ang), jnp.sin(ang)
    f = jnp.concatenate([jnp.concatenate([c, s], 1), jnp.concatenate([-s, c], 1)], 0)
    return f.astype(BF16), f.T.astype(BF16)


def _dense_dft(n, n_in, full_input):
    k = jnp.arange(n, dtype=I32)[:, None]
    t = jnp.arange(n_in, dtype=I32)[None, :]
    ang = (2.0 * math.pi / n) * ((k * t) % n).astype(F32)
    c, s = jnp.cos(ang), jnp.sin(ang)
    if full_input:
        return jnp.concatenate([c, -s], 0).astype(BF16)
    fwd = jnp.concatenate([jnp.concatenate([c, s], 1), jnp.concatenate([-s, c], 1)], 0)
    return fwd.astype(BF16), (fwd.T * (1.0 / n)).astype(BF16)


def fft_tables():
    return _dft_tables(False) + (_dft_tables(True),) + _dft_radix()


def hyena_long(vx, filt, l1, skip, tables):
    _, b, seq, c = vx.shape
    assert b == 2
    r = FFT_RADIX
    inv_l1 = 1.0 / l1[:2]
    if 2 * seq == r * r:
        mats, mats_t, mats_f, fmat, fmat_t = tables
        vx4 = vx.reshape(3, r, r, c)
        filt4 = filt.reshape(2, r, r, c)
        spec = [fft_stage_mid(fft_stage_in(filt4, o, mats_f), fmat, fmat_t, scale=inv_l1[o:o + 1])
                for o in range(2)]
        z4 = fft_stage_out(fft_stage_mid(fft_stage_in(vx4, 0, mats), fmat, fmat_t, h_pk=spec[0]),
                           mats_t, vx4, 0, vx4, 1, skip[0:1], F32)[None]
        y3 = fft_stage_out(fft_stage_mid(fft_stage_in(z4, 0, mats), fmat, fmat_t, h_pk=spec[1]),
                           mats_t, z4, 0, vx4, 2, skip[1:2], BF16)
        return y3.reshape(b, seq, c)
    n = 2 * seq
    fwd, inv = _dense_dft(n, seq, False)
    fwd_f = _dense_dft(n, n, True)
    v, x1, x2 = (vx[i].reshape(b * seq, c) for i in range(3))

    def conv(u, o):
        hs = matmul([fwd_f], filt[o][None], 0, n_out=c) * inv_l1[o:o + 1]
        xs = matmul([fwd], u[None], 0, n_out=c)
        hr, hi, xr, xi = hs[:n], hs[n:], xs[:n], xs[n:]
        ys = jnp.concatenate([xr * hr - xi * hi, xr * hi + xi * hr], 0)
        return matmul([inv], ys[None], 0, n_out=c)

    z = x1 * (conv(v, 0) + v * skip[0:1])
    y = x2 * (conv(z, 1) + z * skip[1:2])
    return y.astype(BF16).reshape(b, seq, c)


def _dwconv_kernel(x_ref, p_ref, n_ref, w_ref, b_ref, g_ref, be_ref, o_ref, win_ref, y_ref, sh_ref):
    i = pl.program_id(1)
    nt = pl.num_programs(1)
    ts = x_ref.shape[1]
    d = x_ref.shape[2]
    win_ref[0:HALO, :] = jnp.where(i == 0, 0.0, p_ref[0])
    win_ref[HALO:HALO + ts, :] = x_ref[0]
    win_ref[HALO + ts:, :] = jnp.where(i == nt - 1, 0.0, n_ref[0])
    pad = (CONF_K - 1) // 2
    first = HALO - pad
    rows = DW_ROWS
    def lane_block(cb, carry):
        cs = pl.ds(pl.multiple_of(cb * LANES, LANES), LANES)
        wb = [jnp.broadcast_to(w_ref[j:j + 1, cs], (SUBLANES, LANES)) for j in range(CONF_K)]
        bb = jnp.broadcast_to(b_ref[:, cs], (SUBLANES, LANES))
        for r0 in range(0, ts, rows):
            acc = jnp.zeros((rows // SUBLANES, SUBLANES, LANES), F32)
            for r in range(SUBLANES):
                offs = [o for o in range(first, first + CONF_K) if o % SUBLANES == r]
                span = rows + (max(offs) // SUBLANES) * SUBLANES
                sh_ref[r, 0:span, :] = win_ref[pl.ds(r0 + r, span), cs]
                for o in offs:
                    a8 = (o // SUBLANES) * SUBLANES
                    tap = sh_ref[r, a8:a8 + rows, :].reshape(rows // SUBLANES, SUBLANES, LANES)
                    acc = acc + tap * wb[o - first]
            y_ref[r0:r0 + rows, cs] = (acc + bb).reshape(rows, LANES)
        return carry

    lax.fori_loop(0, d // LANES, lane_block, 0)
    y = y_ref[...]
    mu = jnp.mean(y, axis=-1, keepdims=True)
    yc = y - mu
    var = jnp.mean(yc * yc, axis=-1, keepdims=True)
    a = yc * lax.rsqrt(var + NORM_EPS) * g_ref[...] + be_ref[...]
    o_ref[0] = (a * jax.nn.sigmoid(a)).astype(o_ref.dtype)


def conformer_dwconv(x, w, b, g, be):
    bsz, s, d = x.shape
    ts = LN_ROWS
    hb = ts // HALO
    nhb = s // HALO
    vec = lambda r: BS((r, d), lambda bi, i: (0, 0))
    return pl.pallas_call(
        _dwconv_kernel,
        out_shape=SDS((bsz, s, d), BF16),
        grid=(bsz, s // ts),
        in_specs=[BS((1, ts, d), lambda bi, i: (bi, i, 0)),
                  BS((1, HALO, d), lambda bi, i: (bi, jnp.maximum(i * hb - 1, 0), 0)),
                  BS((1, HALO, d), lambda bi, i: (bi, jnp.minimum((i + 1) * hb, nhb - 1), 0)),
                  vec(CONF_K), vec(1), vec(1), vec(1)],
        out_specs=BS((1, ts, d), lambda bi, i: (bi, i, 0)),
        scratch_shapes=[pltpu.VMEM((ts + 2 * HALO, d), F32), pltpu.VMEM((ts, d), F32),
                        pltpu.VMEM((SUBLANES, DW_ROWS + 2 * HALO, LANES), F32)],
        compiler_params=_params(("parallel", "parallel")),
        name="conformer_dwconv",
    )(x, x, x, w, b.reshape(1, d), g.reshape(1, d), be.reshape(1, d))


def _layer_norm(v, g, b):
    mu = jnp.mean(v, axis=-1, keepdims=True)
    vc = v - mu
    var = jnp.mean(vc * vc, axis=-1, keepdims=True)
    return vc * lax.rsqrt(var + NORM_EPS) * g + b


def _route(tok, rw_ref, rb_ref):
    rows = tok.shape[0]
    logits_t = _dot3(tok, rw_ref[0], rw_ref[1]).T
    aff = jax.nn.sigmoid(logits_t[:N_EXPERTS])
    sel = aff + rb_ref[:N_EXPERTS]
    sc = [sel[e:e + 1] for e in range(N_EXPERTS)]
    ac = [aff[e:e + 1] for e in range(N_EXPERTS)]
    scores = []
    for gi in range(N_GROUPS):
        m = sc[gi * 4:(gi + 1) * 4]
        best = m[0] + m[1]
        for a in range(4):
            for bb in range(a + 1, 4):
                if (a, bb) != (0, 1):
                    best = jnp.maximum(best, m[a] + m[bb])
        scores.append(best)
    top = scores[0]
    grp = jnp.zeros((1, rows), I32)
    for gi in range(1, N_GROUPS):
        upd = scores[gi] > top
        grp = jnp.where(upd, gi, grp)
        top = jnp.where(upd, scores[gi], top)

    def pick(cols, k):
        out = cols[k]
        for gi in range(1, N_GROUPS):
            out = jnp.where(grp == gi, cols[gi * 4 + k], out)
        return out

    ms = [pick(sc, k) for k in range(4)]
    ma = [pick(ac, k) for k in range(4)]

    def argmax4(vals):
        bv, bi, ba = vals[0], jnp.zeros((1, rows), I32), ma[0]
        for k in range(1, 4):
            upd = vals[k] > bv
            bi = jnp.where(upd, k, bi)
            ba = jnp.where(upd, ma[k], ba)
            bv = jnp.where(upd, vals[k], bv)
        return bi, ba

    i1, a1 = argmax4(ms)
    i2, a2 = argmax4([jnp.where(i1 == k, -jnp.inf, ms[k]) for k in range(4)])
    tot = a1 + a2
    fields = [(grp * 4 + i1).astype(F32), (grp * 4 + i2).astype(F32), a1 / tot, a2 / tot]
    route_t = jnp.concatenate(fields + [jnp.zeros((LANES - len(fields), rows), F32)], axis=0)
    return route_t.T


def _ln_router_kernel(x_ref, y_ref, g_ref, lg_ref, lb_ref, sc_ref, sh_ref, rw_ref, rb_ref,
                      xo_ref, tok_ref, rt_ref):
    v = DEEPNORM_ALPHA * x_ref[0] + g_ref[0] * y_ref[0].astype(F32)
    xn = _layer_norm(v, lg_ref[...], lb_ref[...])
    xo_ref[0] = xn
    tok = xn * (1.0 + sc_ref[0]) + sh_ref[0]
    half = tok.shape[1] // 2
    tok_ref[0] = _pack(tok[:, :half], tok[:, half:])
    rt_ref[0] = _route(tok, rw_ref, rb_ref)


def ln_router(x, y, gate, ln_g, ln_b, sc, sh, rw_pad, rb_pad):
    b, s, d = x.shape
    ts = min(LN_ROWS, s)
    tile = BS((1, ts, d), lambda bi, i: (bi, i, 0))
    vec = BS((1, 1, d), lambda bi, i: (bi, 0, 0))
    par = BS((1, d), lambda bi, i: (0, 0))
    return pl.pallas_call(
        _ln_router_kernel,
        out_shape=[SDS((b, s, d), F32), SDS((b, s, d // 2), U32), SDS((b, s, LANES), F32)],
        grid=(b, s // ts),
        in_specs=[tile, tile, vec, par, par, vec, vec,
                  BS((2, d, LANES), lambda bi, i: (0, 0, 0)), BS((LANES, ts), lambda bi, i: (0, 0))],
        out_specs=[tile, BS((1, ts, d // 2), lambda bi, i: (bi, i, 0)), BS((1, ts, LANES), lambda bi, i: (bi, i, 0))],
        compiler_params=_params(("parallel", "parallel")),
        name="ln_router",
    )(x, y.reshape(b, s, d), gate, ln_g.reshape(1, d), ln_b.reshape(1, d), sc, sh, rw_pad, rb_pad)


def _row_copy_wait(src_rows_ref, dst_any, sem, n_rows):
    pltpu.make_async_copy(src_rows_ref, dst_any.at[pl.ds(0, n_rows), :], sem).wait()


def _dispatch_kernel(pos_ref, lt_ref, na_ref, lat_ref, ctx_ref, xs_hbm, zero_ref, sem, zsem, *, n_lat, n_tok):
    i = pl.program_id(0)
    ts = lat_ref.shape[0]
    n_tiles = xs_hbm.shape[0] // GMM_ROWS

    def zero_tile(t):
        start = pl.multiple_of(t * GMM_ROWS, GMM_ROWS)
        return pltpu.make_async_copy(zero_ref, xs_hbm.at[pl.ds(start, GMM_ROWS), :], zsem)

    @pl.when(i == 0)
    def _():
        zero_ref[...] = jnp.zeros_like(zero_ref)
        for e in range(N_EXPERTS):
            @pl.when(lt_ref[e] >= 0)
            def _():
                zero_tile(lt_ref[e]).start()

        def start_tail(t, c):
            zero_tile(t).start()
            return c

        lax.fori_loop(na_ref[0], n_tiles, start_tail, 0)
        for e in range(N_EXPERTS):
            @pl.when(lt_ref[e] >= 0)
            def _():
                zero_tile(0).wait()

        def wait_tail(t, c):
            zero_tile(0).wait()
            return c

        lax.fori_loop(na_ref[0], n_tiles, wait_tail, 0)

    def scatter(src_ref):
        base = i * ts

        def body(r, c):
            for k in range(2):
                dst = pos_ref[k * n_tok + base + r]
                pltpu.make_async_copy(src_ref.at[pl.ds(r, 1), :], xs_hbm.at[pl.ds(dst, 1), :],
                                      sem).start(priority=k)
            return c

        lax.fori_loop(0, ts, body, 0, unroll=ROW_COPY_UNROLL)
        for _ in range(2):
            _row_copy_wait(src_ref, xs_hbm, sem, ts)

    if ctx_ref is None:
        scatter(lat_ref)
    else:
        @pl.when(i < n_lat)
        def _():
            scatter(lat_ref)

        @pl.when(i >= n_lat)
        def _():
            scatter(ctx_ref)


def dispatch(pos, last_tile, n_active, tok_lat, tok_ctx, n_rows):
    t_lat, d = tok_lat.shape
    ts = LN_ROWS
    n_lat = t_lat // ts
    n_ctx = 0 if tok_ctx is None else tok_ctx.shape[0] // ts
    n_tok = t_lat + (0 if tok_ctx is None else tok_ctx.shape[0])
    in_specs = [BS((ts, d), lambda i, p, l, na: (jnp.minimum(i, n_lat - 1), 0))]
    args = [tok_lat]
    if tok_ctx is not None:
        in_specs.append(BS((ts, d), lambda i, p, l, na: (jnp.maximum(i - n_lat, 0), 0)))
        args.append(tok_ctx)

    def kern(pos_ref, lt_ref, na_ref, *refs):
        if tok_ctx is None:
            lat_ref, xs_hbm, zero_ref, sem, zsem = refs
            ctx_ref = None
        else:
            lat_ref, ctx_ref, xs_hbm, zero_ref, sem, zsem = refs
        _dispatch_kernel(pos_ref, lt_ref, na_ref, lat_ref, ctx_ref, xs_hbm, zero_ref, sem, zsem,
                         n_lat=n_lat, n_tok=n_tok)

    return pl.pallas_call(
        kern,
        out_shape=SDS((n_rows, d), tok_lat.dtype),
        grid_spec=pltpu.PrefetchScalarGridSpec(
            num_scalar_prefetch=3, grid=(n_lat + n_ctx,),
            in_specs=in_specs,
            out_specs=BS(memory_space=pl.ANY),
            scratch_shapes=[pltpu.VMEM((GMM_ROWS, d), tok_lat.dtype), pltpu.SemaphoreType.DMA(()),
                            pltpu.SemaphoreType.DMA(())]),
        compiler_params=_params(("arbitrary",)),
        name="dispatch",
    )(pos, last_tile, n_active, *args)


def _gmm_kernel(te_ref, na_ref, x_ref, w1_ref, w3_ref, w2_ref, o_ref, acc_ref):
    i = pl.program_id(0)
    j = pl.program_id(1)
    dh = x_ref.shape[1]

    def half_mlp():
        xa, xb = _unpack(x_ref[...])
        xa = xa.astype(BF16)
        xb = xb.astype(BF16)

        def up(w_ref):
            return (jnp.dot(xa, w_ref[0, 0, :dh, :].astype(BF16), preferred_element_type=F32)
                    + jnp.dot(xb, w_ref[0, 0, dh:, :].astype(BF16), preferred_element_type=F32))

        a = up(w1_ref)
        b = up(w3_ref)
        h = (a * jax.nn.sigmoid(a) * b).astype(BF16)
        return jnp.dot(h, w2_ref[0, 0].astype(BF16), preferred_element_type=F32)

    @pl.when(jnp.logical_and(i < na_ref[0], j == 0))
    def _():
        acc_ref[...] = half_mlp()

    @pl.when(jnp.logical_and(i < na_ref[0], j != 0))
    def _():
        tot = acc_ref[...] + half_mlp()
        o_ref[...] = _pack(tot[:, :dh], tot[:, dh:])

    @pl.when(jnp.logical_and(i >= na_ref[0], j == 0))
    def _():
        o_ref[...] = jnp.zeros_like(o_ref)


def grouped_mlp(tile_expert, n_active, xs, w1, w3, w2, layer):
    n_rows, dh = xs.shape
    d = 2 * dh
    n_tiles = n_rows // GMM_ROWS

    def tile(i, te, na):
        return jnp.minimum(i, na[0] - 1)

    def half(i, j, na):
        last = na[0] - 1
        return jnp.where(i <= last, jnp.where(i % 2 == 0, j, 1 - j), jnp.where(last % 2 == 0, 1, 0))

    def expert(i, te, na):
        return te[tile(i, te, na)]

    return pl.pallas_call(
        _gmm_kernel,
        out_shape=SDS((n_rows, dh), U32),
        grid_spec=pltpu.PrefetchScalarGridSpec(
            num_scalar_prefetch=2, grid=(n_tiles, 2),
            in_specs=[BS((GMM_ROWS, dh), lambda i, j, te, na: (tile(i, te, na), 0)),
                      BS((1, 1, d, GMM_HALF), lambda i, j, te, na: (layer, expert(i, te, na), 0, half(i, j, na))),
                      BS((1, 1, d, GMM_HALF), lambda i, j, te, na: (layer, expert(i, te, na), 0, half(i, j, na))),
                      BS((1, 1, GMM_HALF, d), lambda i, j, te, na: (layer, expert(i, te, na), half(i, j, na), 0))],
            out_specs=BS((GMM_ROWS, dh), lambda i, j, te, na: (i, 0)),
            scratch_shapes=[pltpu.VMEM((GMM_ROWS, d), F32)]),
        compiler_params=_params(("arbitrary", "arbitrary")),
        name="grouped_mlp",
    )(tile_expert, n_active, xs, w1, w3, w2)


def _combine_ln_kernel(pos_ref, x_ref, rt_ref, g_ref, lg_ref, lb_ref, sc_ref, sh_ref, ys_hbm,
                       xo_ref, h_ref, buf_ref, sem, *, tok_off, n_tok, tiles_per_batch, emit_h):
    ts = x_ref.shape[1]
    step = pl.program_id(0) * tiles_per_batch + pl.program_id(1)
    n_steps = pl.num_programs(0) * tiles_per_batch
    slot = step % 2

    def gather(tile, dst_slot):
        base = tok_off + tile * ts

        def body(r, c):
            for k in range(2):
                src = pos_ref[k * n_tok + base + r]
                pltpu.make_async_copy(ys_hbm.at[pl.ds(src, 1), :], buf_ref.at[dst_slot, k, pl.ds(r, 1), :],
                                      sem.at[dst_slot]).start(priority=k)
            return c

        lax.fori_loop(0, ts, body, 0, unroll=ROW_COPY_UNROLL)

    @pl.when(step == 0)
    def _():
        gather(step, slot)

    @pl.when(step + 1 < n_steps)
    def _():
        gather(step + 1, 1 - slot)

    for k in range(2):
        pltpu.make_async_copy(ys_hbm.at[pl.ds(0, ts), :], buf_ref.at[slot, k], sem.at[slot]).wait()
    rt = rt_ref[0]
    halves = []
    for part in range(2):
        y0 = _unpack(buf_ref[slot, 0])[part]
        y1 = _unpack(buf_ref[slot, 1])[part]
        halves.append(rt[:, 2:3] * y0 + rt[:, 3:4] * y1)
    moe = jnp.concatenate(halves, axis=1)
    xn = _layer_norm(DEEPNORM_ALPHA * x_ref[0] + g_ref[0] * moe, lg_ref[...], lb_ref[...])
    xo_ref[0] = xn
    if emit_h:
        h_ref[0] = (xn * (1.0 + sc_ref[0]) + sh_ref[0]).astype(h_ref.dtype)
    else:
        h_ref[0] = jnp.zeros(h_ref.shape[1:], h_ref.dtype)


def combine_ln(pos, x, route, gate, ln_g, ln_b, sc, sh, ys, *, tok_off, n_tok, emit_h=True):
    b, s, d = x.shape
    ts = min(LN_ROWS, s)
    tpb = s // ts
    tile = BS((1, ts, d), lambda bi, i, p: (bi, i, 0))
    vec = BS((1, 1, d), lambda bi, i, p: (bi, 0, 0))
    par = BS((1, d), lambda bi, i, p: (0, 0))
    h_rows = ts if emit_h else SUBLANES
    return pl.pallas_call(
        functools.partial(_combine_ln_kernel, tok_off=tok_off, n_tok=n_tok, tiles_per_batch=tpb, emit_h=emit_h),
        out_shape=[SDS((b, s, d), F32), SDS((b, tpb * h_rows, d), BF16)],
        grid_spec=pltpu.PrefetchScalarGridSpec(
            num_scalar_prefetch=1, grid=(b, tpb),
            in_specs=[tile, BS((1, ts, LANES), lambda bi, i, p: (bi, i, 0)), vec, par, par, vec, vec,
                      BS(memory_space=pl.ANY)],
            out_specs=[tile, BS((1, h_rows, d), lambda bi, i, p: (bi, i, 0))],
            scratch_shapes=[pltpu.VMEM((2, 2, ts, d // 2), U32), pltpu.SemaphoreType.DMA((2,))]),
        compiler_params=_params(("arbitrary", "arbitrary")),
        name="combine_ln",
    )(pos, x, route, gate, ln_g.reshape(1, d), ln_b.reshape(1, d), sc, sh, ys)


def _rope_tables(n_tokens):
    rows = n_tokens // GRID_W
    row = jnp.repeat(jnp.arange(rows, dtype=F32), GRID_W)
    col = jnp.tile(jnp.arange(GRID_W, dtype=F32), rows)
    half = HEAD_DIM // 2
    inv = ROPE_THETA ** (-jnp.arange(0, half, 2, dtype=F32) / half)
    ang = jnp.repeat(jnp.concatenate([row[:, None] * inv, col[:, None] * inv], -1), 2, axis=-1)
    cos, sin = jnp.cos(ang), jnp.sin(ang)
    even = (jnp.arange(HEAD_DIM) % 2 == 0)[None, :]
    return cos, jnp.where(even, -sin, 0.0), jnp.where(even, 0.0, sin)


def _even_mixer(h, hc, ctx_out, rope_tables, fft_tables, w_in_all, q_g, k_g, conv_w, conv_b,
                f_w1, f_b1, f_w2, f_b2, f_w3, f_freq, skip, w_out_all, li):
    b, s, d = h.shape
    lc = hc.shape[1]
    qg = q_g.reshape(1, HEAD_DIM)
    kg = k_g.reshape(1, HEAD_DIM)
    n_in = w_in_all.shape[2]

    proj = matmul([h.reshape(b * s, d)], w_in_all, li, n_out=n_in, tn=768).reshape(b, s, n_in)
    proj_c = matmul([hc.reshape(b * lc, d)], w_in_all, li, n_out=n_in, tn=768).reshape(b, lc, n_in)
    q, k, v = qkv_prepare(proj, qg, kg, rope_tables, with_q=True)
    qc, kc, vc = qkv_prepare(proj_c, qg, kg, None, with_q=ctx_out)
    att = attention(q, jnp.concatenate([k, kc], axis=1), jnp.concatenate([v, vc], axis=1))

    def hyena(p, seq):
        vx = hyena_short_conv(p, conv_w, conv_b)
        filt, l1 = hyena_filter(seq, f_w1, f_b1, f_w2, f_b2, f_w3, f_freq)
        return hyena_long(vx, filt, l1, skip, fft_tables)

    hy = hyena(proj, s)
    y = matmul([att.reshape(b * s, ATTN_WIDTH), hy.reshape(b * s, HYENA_WIDTH)], w_out_all, li,
               n_out=d).reshape(b, s, d)
    if not ctx_out:
        return y, None
    att_c = attention(qc, kc, vc)
    hy_c = hyena(proj_c, lc)
    yc = matmul([att_c.reshape(b * lc, ATTN_WIDTH), hy_c.reshape(b * lc, HYENA_WIDTH)], w_out_all, li,
                n_out=d).reshape(b, lc, d)
    return y, yc


def _conformer(h, in_w, in_b, dw_w, dw_b, n_g, n_b, out_w, out_b, li):
    b, s, d = h.shape
    a = matmul([h.reshape(b * s, d)], in_w, li, n_out=d, bias=in_b, glu=True).reshape(b, s, d)
    a = conformer_dwconv(a, dw_w[li], dw_b[li], n_g[li], n_b[li])
    return matmul([a.reshape(b * s, d)], out_w, li, n_out=d, bias=out_b).reshape(b, s, d)


def _dispatch_plan(experts, n_tiles):
    n_tok = experts.shape[0]
    flat = experts.T.reshape(-1)
    hit = flat[:, None] == jnp.arange(N_EXPERTS, dtype=I32)[None, :]
    blk = LANES
    oh = hit.astype(F32).reshape(-1, blk, N_EXPERTS)
    strict_lower = (jnp.arange(blk)[:, None] > jnp.arange(blk)[None, :]).astype(F32)
    within = jnp.einsum("ij,bje->bie", strict_lower, oh, precision=HIGHEST)
    blk_tot = jnp.sum(oh, axis=1)
    blk_off = jnp.cumsum(blk_tot, axis=0) - blk_tot
    before = (within + blk_off[:, None, :]).reshape(-1, N_EXPERTS)
    rank = jnp.sum(jnp.where(hit, before, 0.0), axis=1).astype(I32)
    counts = jnp.sum(blk_tot, axis=0).astype(I32)
    padded = ((counts + GMM_ROWS - 1) // GMM_ROWS) * GMM_ROWS
    ends = jnp.cumsum(padded)
    starts = ends - padded
    pos = (starts[flat] + rank).astype(I32)
    n_active = (ends[-1] // GMM_ROWS).astype(I32).reshape(1)
    tile_start = jnp.arange(n_tiles, dtype=I32) * GMM_ROWS
    tile_expert = jnp.minimum(jnp.sum((ends[None, :] <= tile_start[:, None]).astype(I32), axis=1),
                              N_EXPERTS - 1).astype(I32)
    last_tile = jnp.where(padded > 0, ends // GMM_ROWS - 1, -1).astype(I32)
    del n_tok
    return pos, n_active, tile_expert, last_tile


def kernel(x, c, ctx, c_ctx, mod_w, mod_b, ln_g, ln_b, mix_in_w, q_norm_g, k_norm_g, hy_conv_w, hy_conv_b,
           hy_w1, hy_b1, hy_w2, hy_b2, hy_w3, hy_freq, hy_skip, mix_out_w, cf_in_w, cf_in_b, cf_dw_w, cf_dw_b,
           cf_ln_g, cf_ln_b, cf_out_w, cf_out_b, router_w, router_b, exp_w1, exp_w3, exp_w2):
    b, s, d = x.shape
    lc = ctx.shape[1]
    rope_tables = _rope_tables(s)
    dft = fft_tables()
    cvec =jnp.concatenate([c, c_ctx[None], jnp.zeros((SUBLANES - b - 1, d), F32)], 0)
    mods = mod_vectors(cvec, mod_w, mod_b)
    rw_pad = jnp.stack(_split(jnp.concatenate([router_w, jnp.zeros((d, LANES - N_EXPERTS), F32)], 1)))
    rb_pad = jnp.broadcast_to(jnp.concatenate([router_b, jnp.zeros((LANES - N_EXPERTS,), F32)])[:, None],
                              (LANES, LN_ROWS))

    def mod_of(l, j, is_ctx):
        m = mods[l, :, j * d:(j + 1) * d]
        rows = jnp.broadcast_to(m[b:b + 1], (b, d)) if is_ctx else m[:b]
        return rows.reshape(b, 1, d)

    h = modulate(x, mod_of(0, 1, False), mod_of(0, 0, False))
    hc = modulate(ctx, mod_of(0, 1, True), mod_of(0, 0, True))
    for l in range(DEPTH):
        ctx_needed = any(j > l and j % 2 == 0 for j in range(DEPTH))
        li = l // 2
        if l % 2 == 0:
            y, yc = _even_mixer(h, hc, ctx_needed, rope_tables, dft, mix_in_w, q_norm_g[li], k_norm_g[li],
                                hy_conv_w[li], hy_conv_b[li], hy_w1[li], hy_b1[li], hy_w2[li], hy_b2[li],
                                hy_w3[li], hy_freq[li], hy_skip[li], mix_out_w, li)
        else:
            cf = (cf_in_w, cf_in_b, cf_dw_w, cf_dw_b, cf_ln_g, cf_ln_b, cf_out_w, cf_out_b)
            y = _conformer(h, *cf, li)
            yc = _conformer(hc, *cf, li) if ctx_needed else None

        x, tok, route = ln_router(x, y, mod_of(l, 2, False), ln_g[l, 0], ln_b[l, 0],
                                  mod_of(l, 4, False), mod_of(l, 3, False), rw_pad, rb_pad)
        routes = [route.reshape(b * s, LANES)]
        tok_c = None
        if ctx_needed:
            ctx, tok_c, route_c = ln_router(ctx, yc, mod_of(l, 2, True), ln_g[l, 0], ln_b[l, 0],
                                            mod_of(l, 4, True), mod_of(l, 3, True), rw_pad, rb_pad)
            routes.append(route_c.reshape(b * lc, LANES))
            tok_c = tok_c.reshape(b * lc, d // 2)
        n_tok = b * s + (b * lc if ctx_needed else 0)
        n_tiles = (2 * n_tok) // GMM_ROWS + N_EXPERTS
        experts = jnp.concatenate([r[:, :2] for r in routes], 0).astype(I32)
        pos, n_active, tile_expert, last_tile = _dispatch_plan(experts, n_tiles)
        xs = dispatch(pos, last_tile, n_active, tok.reshape(b * s, d // 2), tok_c, n_tiles * GMM_ROWS)
        ys = grouped_mlp(tile_expert, n_active, xs, exp_w1, exp_w3, exp_w2, l)

        last = l == DEPTH - 1
        nl = min(l + 1, DEPTH - 1)
        x, h = combine_ln(pos, x, route, mod_of(l, 5, False), ln_g[l, 1], ln_b[l, 1],
                          mod_of(nl, 1, False), mod_of(nl, 0, False), ys, tok_off=0, n_tok=n_tok,
                          emit_h=not last)
        if ctx_needed:
            ctx, hc = combine_ln(pos, ctx, route_c, mod_of(l, 5, True), ln_g[l, 1], ln_b[l, 1],
                                 mod_of(nl, 1, True), mod_of(nl, 0, True), ys, tok_off=b * s, n_tok=n_tok)
    return x
```
